```python
import jax, jax.numpy as jnp
from jax import lax
import numpy as np

D_MODEL = 1024
BATCH = 8
SEQ = 2048
DEPTH = 4
DEC_BATCH = 128
DEC_SEQ = 4
PAST_LEN = 16384
PAGE_SIZE = 128

N_MIXERS = 4
D_FF = 2816
FFN_HALF = 0.5
EPS = 1e-6
CONV_A_WIDTH = 31
CONV_B_WIDTH = 3
GMLP_WIDTH = D_MODEL
CHUNK = 128
C_HEADS = 8
C_HEAD_DIM = GMLP_WIDTH // C_HEADS
POOL_WINDOWS = (2, 4, 8, 16)
POOL_GROUPS = len(POOL_WINDOWS)
POOL_GROUP_DIM = D_MODEL // POOL_GROUPS
POOL_PAD = max(POOL_WINDOWS) - 1

kernel_name = "hybrid_conv_gmlp_pool_decoder_step"


def rmsnorm(x, g):
    xf = x.astype(jnp.float32)
    y = xf * lax.rsqrt(jnp.mean(xf * xf, axis=-1, keepdims=True) + EPS)
    return (y * g.astype(jnp.float32)).astype(x.dtype)


def layernorm(x, g, b):
    xf = x.astype(jnp.float32)
    xc = xf - jnp.mean(xf, axis=-1, keepdims=True)
    y = xc * lax.rsqrt(jnp.mean(xc * xc, axis=-1, keepdims=True) + EPS)
    return (y * g.astype(jnp.float32) + b.astype(jnp.float32)).astype(x.dtype)


def depthwise_conv(ext, w):
    return lax.conv_general_dilated(
        ext, w.astype(ext.dtype)[:, None, :], window_strides=(1,), padding="VALID",
        dimension_numbers=("NWC", "WIO", "NWC"), feature_group_count=ext.shape[-1])


def swiglu(x, w_in, w_down):
    a = x @ w_in
    gate, up = jnp.split(a, 2, axis=-1)
    return (jax.nn.silu(gate) * up) @ w_down


def conformer_conv(x, buf, w_in, conv_w, conv_b, ln_g, ln_b, w_out):
    a = x @ w_in
    glu = a[..., :D_MODEL] * jax.nn.sigmoid(a[..., D_MODEL:])
    ext = jnp.concatenate([buf.astype(glu.dtype), glu], axis=1)
    c = depthwise_conv(ext, conv_w) + conv_b
    c = jax.nn.silu(layernorm(c, ln_g, ln_b))
    return c @ w_out, ext[:, -(CONV_A_WIDTH - 1):]


def short_gated_conv(x, buf, w_in, conv_w, w_out):
    p = x @ w_in
    b_gate, c_gate, xin = jnp.split(p, 3, axis=-1)
    z = c_gate * xin
    ext = jnp.concatenate([buf.astype(z.dtype), z], axis=1)
    c = depthwise_conv(ext, conv_w)
    return (b_gate * c) @ w_out, ext[:, -(CONV_B_WIDTH - 1):]


def chunk_spatial(v, ws, bs):
    bsz, length, _ = v.shape
    pad = (-length) % CHUNK
    vp = jnp.pad(v, ((0, 0), (0, pad), (0, 0)))
    n_chunks = (length + pad) // CHUNK
    vp = vp.reshape(bsz, n_chunks, CHUNK, C_HEADS, C_HEAD_DIM)
    mask = jnp.tril(jnp.ones((CHUNK, CHUNK), dtype=bool))
    wm = jnp.where(mask[None], ws, jnp.zeros_like(ws))
    out = jnp.einsum("hij,bnjhc->bnihc", wm, vp) + bs.T[None, None, :, :, None]
    return out.reshape(bsz, n_chunks * CHUNK, GMLP_WIDTH)[:, :length]


def chunk_gmlp(x, w_in, ln_g, ln_b, ws, bs, w_out):
    z = jax.nn.gelu(x @ w_in, approximate=False)
    u, v = jnp.split(z, 2, axis=-1)
    v = layernorm(v, ln_g, ln_b)
    s = chunk_spatial(v, ws, bs)
    return (u * s) @ w_out, v


def multiscale_pool(x, buf, pos0, w_group, scale):
    bsz, length, _ = x.shape
    ext = jnp.concatenate([buf.astype(x.dtype), x], axis=1)
    xf = ext.astype(jnp.float32)
    cs = jnp.concatenate([jnp.zeros((bsz, 1, D_MODEL), jnp.float32), jnp.cumsum(xf, axis=1)], axis=1)
    end = cs[:, POOL_PAD + 1:]
    pos = pos0 + jnp.arange(length)
    pooled = []
    for g, w in enumerate(POOL_WINDOWS):
        sl = slice(g * POOL_GROUP_DIM, (g + 1) * POOL_GROUP_DIM)
        start = cs[:, POOL_PAD + 1 - w: POOL_PAD + 1 - w + length, sl]
        cnt = jnp.minimum(w, pos + 1).astype(jnp.float32)[None, :, None]
        pooled.append((end[..., sl] - start) / cnt)
    diff = jnp.concatenate(pooled, axis=-1) - x.astype(jnp.float32)
    diff = diff.astype(x.dtype).reshape(bsz, length, POOL_GROUPS, POOL_GROUP_DIM)
    y = jnp.einsum("blgc,gcd->blgd", diff, w_group).reshape(bsz, length, D_MODEL)
    return y * scale, ext[:, -POOL_PAD:]


def trunk(x, buf_a, buf_b, buf_pool, pos0, norm_g, ffn_w_in, ffn_w_down,
          a_w_in, a_conv_w, a_conv_b, a_ln_g, a_ln_b, a_w_out,
          b_w_in, b_conv_w, b_w_out,
          c_w_in, c_ln_g, c_ln_b, c_ws, c_bs, c_w_out,
          d_w_group, d_scale):
    h = x
    st_a = st_b = st_c = st_d = None
    for i in range(DEPTH):
        g = norm_g[i]
        f = swiglu(rmsnorm(h, g[0]), ffn_w_in[i, 0], ffn_w_down[i, 0])
        h = h + FFN_HALF * rmsnorm(f, g[1])
        xm = rmsnorm(h, g[2])
        kind = i % N_MIXERS
        if kind == 0:
            m, st_a = conformer_conv(xm, buf_a, a_w_in, a_conv_w, a_conv_b, a_ln_g, a_ln_b, a_w_out)
        elif kind == 1:
            m, st_b = short_gated_conv(xm, buf_b, b_w_in, b_conv_w, b_w_out)
        elif kind == 2:
            m, st_c = chunk_gmlp(xm, c_w_in, c_ln_g, c_ln_b, c_ws, c_bs, c_w_out)
        else:
            m, st_d = multiscale_pool(xm, buf_pool, pos0, d_w_group, d_scale)
        h = h + rmsnorm(m, g[3])
        f = swiglu(rmsnorm(h, g[4]), ffn_w_in[i, 1], ffn_w_down[i, 1])
        h = h + FFN_HALF * rmsnorm(f, g[5])
    return h, st_a, st_b, st_c, st_d


def setup_inputs(seed: int = 0) -> dict:
    key = jax.random.key(seed)
    ks = jax.random.split(key, 26)

    def nrm(k, shape, scale):
        return jax.random.normal(k, shape, jnp.float32) * scale

    D = D_MODEL
    return {
        "x_prompt": nrm(ks[0], (BATCH, SEQ, D), 1.0),
        "x_sample": nrm(ks[1], (DEC_BATCH, DEC_SEQ, D), 1.0),
        "state_conv_a": nrm(ks[2], (DEC_BATCH, CONV_A_WIDTH - 1, D), 0.5),
        "state_conv_b": nrm(ks[3], (DEC_BATCH, CONV_B_WIDTH - 1, D), 0.5),
        "state_pool": nrm(ks[4], (DEC_BATCH, POOL_PAD, D), 1.0),
        "norm_g": 1.0 + nrm(ks[5], (DEPTH, 6, D), 0.05),
        "ffn_w_in": nrm(ks[6], (DEPTH, 2, D, 2 * D_FF), D ** -0.5),
        "ffn_w_down": nrm(ks[7], (DEPTH, 2, D_FF, D), D_FF ** -0.5),
        "a_w_in": nrm(ks[8], (D, 2 * D), D ** -0.5),
        "a_conv_w": nrm(ks[9], (CONV_A_WIDTH, D), CONV_A_WIDTH ** -0.5),
        "a_conv_b": nrm(ks[10], (D,), 0.02),
        "a_ln_g": 1.0 + nrm(ks[11], (D,), 0.05),
        "a_ln_b": nrm(ks[12], (D,), 0.02),
        "a_w_out": nrm(ks[13], (D, D), D ** -0.5),
        "b_w_in": nrm(ks[14], (D, 3 * D), D ** -0.5),
        "b_conv_w": nrm(ks[15], (CONV_B_WIDTH, D), CONV_B_WIDTH ** -0.5),
        "b_w_out": nrm(ks[16], (D, D), D ** -0.5),
        "c_w_in": nrm(ks[17], (D, 2 * GMLP_WIDTH), D ** -0.5),
        "c_ln_g": 1.0 + nrm(ks[18], (GMLP_WIDTH,), 0.05),
        "c_ln_b": nrm(ks[19], (GMLP_WIDTH,), 0.02),
        "c_ws": nrm(ks[20], (C_HEADS, CHUNK, CHUNK), CHUNK ** -0.5),
        "c_bs": 1.0 + nrm(ks[21], (C_HEADS, CHUNK), 0.1),
        "c_w_out": nrm(ks[22], (GMLP_WIDTH, D), GMLP_WIDTH ** -0.5),
        "d_w_group": nrm(ks[23], (POOL_GROUPS, POOL_GROUP_DIM, POOL_GROUP_DIM), POOL_GROUP_DIM ** -0.5),
        "d_scale": 1.0 + nrm(ks[24], (D,), 0.1),
    }


def reference(x_prompt, x_sample, state_conv_a, state_conv_b, state_pool,
              norm_g, ffn_w_in, ffn_w_down,
              a_w_in, a_conv_w, a_conv_b, a_ln_g, a_ln_b, a_w_out,
              b_w_in, b_conv_w, b_w_out,
              c_w_in, c_ln_g, c_ln_b, c_ws, c_bs, c_w_out,
              d_w_group, d_scale):
    weights = (norm_g, ffn_w_in, ffn_w_down,
               a_w_in, a_conv_w, a_conv_b, a_ln_g, a_ln_b, a_w_out,
               b_w_in, b_conv_w, b_w_out,
               c_w_in, c_ln_g, c_ln_b, c_ws, c_bs, c_w_out,
               d_w_group, d_scale)
    bp = x_prompt.shape[0]
    zero_a = jnp.zeros((bp, CONV_A_WIDTH - 1, D_MODEL), x_prompt.dtype)
    zero_b = jnp.zeros((bp, CONV_B_WIDTH - 1, D_MODEL), x_prompt.dtype)
    zero_pool = jnp.zeros((bp, POOL_PAD, D_MODEL), x_prompt.dtype)
    y_prompt, sa_p, sb_p, _, sp_p = trunk(x_prompt, zero_a, zero_b, zero_pool, 0, *weights)
    y_sample, sa_s, sb_s, sc_s, sp_s = trunk(x_sample, state_conv_a, state_conv_b, state_pool, PAST_LEN, *weights)
    return (y_prompt, y_sample, sa_p, sa_s, sb_p, sb_s, sc_s, sp_p, sp_s)
```

```python
import functools

import jax
import jax.numpy as jnp
from jax import lax
from jax.experimental import pallas as pl
from jax.experimental.pallas import tpu as pltpu

D = 1024
D_FF = 2816
DEPTH = 4
EPS = 1e-6
FFN_HALF = 0.5
CONV_A = 31
CONV_B = 3
CHUNK = 128
HEADS = 8
POOL_WINDOWS = (2, 4, 8, 16)
POOL_GD = D // len(POOL_WINDOWS)
POOL_PAD = 15

BATCH = 8
SEQ = 2048
DEC_BATCH = 128
DEC_SEQ = 4

TM = 512
TILES_PER_SEQ = SEQ // TM
N_PROMPT_TILES = BATCH * SEQ // TM
N_TILES = N_PROMPT_TILES + 1
ROWS = N_TILES * TM
FF_CHUNK = 256
CONV_RB = 32
SUBLANES = 8

_BF = jnp.bfloat16
_F32 = jnp.float32
_VMEM_LIMIT = 60 * 1024 * 1024


def _dot(a, b):
    return jnp.dot(a, b, preferred_element_type=_F32)


def _rms(x, g):
    return x * lax.rsqrt(jnp.mean(x * x, axis=-1, keepdims=True) + EPS) * g


def _ln(x, g, b):
    xc = x - jnp.mean(x, axis=-1, keepdims=True)
    return xc * lax.rsqrt(jnp.mean(xc * xc, axis=-1, keepdims=True) + EPS) * g + b


def _gelu(x):
    return 0.5 * x * (1.0 + lax.erf(x * (2.0 ** -0.5)))


def _const_spec(shape):
    zeros = (0,) * len(shape)
    return pl.BlockSpec(shape, lambda i: zeros, pipeline_mode=pl.Buffered(1))


def _tile_spec():
    return pl.BlockSpec((TM, D), lambda i: (i, 0))


def _prompt_tile_index(i):
    return jnp.minimum(i, N_PROMPT_TILES - 1)


def _params():
    return pltpu.CompilerParams(dimension_semantics=("arbitrary",),
                                vmem_limit_bytes=_VMEM_LIMIT)


def _ffn_math(x, g_ref, w_in_ref, w_down_ref):
    xn = _rms(x, g_ref[0:1, :]).astype(_BF)
    acc = None
    for c in range(D_FF // FF_CHUNK):
        c0 = c * FF_CHUNK
        gate = _dot(xn, w_in_ref[:, c0:c0 + FF_CHUNK])
        up = _dot(xn, w_in_ref[:, D_FF + c0:D_FF + c0 + FF_CHUNK])
        act = (gate * jax.nn.sigmoid(gate) * up).astype(_BF)
        d = _dot(act, w_down_ref[c0:c0 + FF_CHUNK, :])
        acc = d if acc is None else acc + d
    return x + FFN_HALF * _rms(acc, g_ref[1:2, :])


def _ffn_body(h_ref, g_ref, w_in_ref, w_down_ref, o_ref):
    o_ref[...] = _ffn_math(h_ref[...], g_ref, w_in_ref, w_down_ref)


def _ffn_first_body(xp_ref, xs_ref, g_ref, w_in_ref, w_down_ref, o_ref):
    is_prompt = pl.program_id(0) < N_PROMPT_TILES
    x = jnp.where(is_prompt, xp_ref[...], xs_ref[...])
    o_ref[...] = _ffn_math(x, g_ref, w_in_ref, w_down_ref)


def _ffn_last_body(h_ref, g_ref, w_in_ref, w_down_ref, yp_ref, ys_ref):
    y = _ffn_math(h_ref[...], g_ref, w_in_ref, w_down_ref)
    i = pl.program_id(0)

    @pl.when(i < N_PROMPT_TILES)
    def _():
        yp_ref[...] = y

    @pl.when(i == N_PROMPT_TILES)
    def _():
        ys_ref[...] = y


def _ffn_weight_specs(layer, half):
    return [
        pl.BlockSpec((None, None, D, 2 * D_FF), lambda i: (layer, half, 0, 0),
                     pipeline_mode=pl.Buffered(1)),
        pl.BlockSpec((None, None, D_FF, D), lambda i: (layer, half, 0, 0),
                     pipeline_mode=pl.Buffered(1)),
    ]


def _ffn(h, g2, w_in, w_down, layer, half):
    return pl.pallas_call(
        _ffn_body,
        grid=(N_TILES,),
        in_specs=[_tile_spec(), _const_spec((2, D))] + _ffn_weight_specs(layer, half),
        out_specs=_tile_spec(),
        out_shape=jax.ShapeDtypeStruct((ROWS, D), _F32),
        compiler_params=_params(),
        name=f"ffn_{layer}_{half}",
    )(h, g2, w_in, w_down)


def _ffn_first(xp, xs, g2, w_in, w_down):
    return pl.pallas_call(
        _ffn_first_body,
        grid=(N_TILES,),
        in_specs=[pl.BlockSpec((TM, D), lambda i: (_prompt_tile_index(i), 0)),
                  pl.BlockSpec((TM, D), lambda i: (0, 0)),
                  _const_spec((2, D))] + _ffn_weight_specs(0, 0),
        out_specs=_tile_spec(),
        out_shape=jax.ShapeDtypeStruct((ROWS, D), _F32),
        compiler_params=_params(),
        name="ffn_first",
    )(xp, xs, g2, w_in, w_down)


def _ffn_last(h, g2, w_in, w_down):
    return pl.pallas_call(
        _ffn_last_body,
        grid=(N_TILES,),
        in_specs=[_tile_spec(), _const_spec((2, D))] + _ffn_weight_specs(DEPTH - 1, 1),
        out_specs=[pl.BlockSpec((TM, D), lambda i: (_prompt_tile_index(i), 0)),
                   pl.BlockSpec((TM, D), lambda i: (0, 0))],
        out_shape=[jax.ShapeDtypeStruct((N_PROMPT_TILES * TM, D), _F32),
                   jax.ShapeDtypeStruct((TM, D), _F32)],
        compiler_params=_params(),
        name="ffn_last",
    )(h, g2, w_in, w_down)


_A_HIST = 32


def _mix_a_body(h_ref, g_ref, w_in_ref, cw_ref, cb_ref, lng_ref, lnb_ref, w_out_ref, st_ref,
                o_ref, sap_ref, glu_s_ref, ext_ref, cwb_ref, c_ref):
    i = pl.program_id(0)

    @pl.when(i == 0)
    def _():
        for k in range(CONV_A):
            cwb_ref[k] = jnp.broadcast_to(cw_ref[k:k + 1, :], (SUBLANES, D))

    x = h_ref[...]
    xm = _rms(x, g_ref[0:1, :]).astype(_BF)
    a = _dot(xm, w_in_ref[...])
    glu = a[:, :D] * jax.nn.sigmoid(a[:, D:])
    bias = cb_ref[...]

    @pl.when(i < N_PROMPT_TILES)
    def _():
        @pl.when(i % TILES_PER_SEQ == 0)
        def _():
            ext_ref[0:_A_HIST, :] = jnp.zeros((_A_HIST, D), _F32)

        ext_ref[_A_HIST:_A_HIST + TM, :] = glu
        first = _A_HIST - (CONV_A - 1)
        for rb in range(TM // CONV_RB):
            r0 = rb * CONV_RB
            acc = jnp.broadcast_to(bias, (CONV_RB, D)).reshape(CONV_RB // SUBLANES, SUBLANES, D)
            for k in range(CONV_A):
                s = r0 + first + k
                rows = ext_ref[s:s + CONV_RB, :].reshape(CONV_RB // SUBLANES, SUBLANES, D)
                acc = acc + rows * cwb_ref[k][None]
            c_ref[r0:r0 + CONV_RB, :] = acc.reshape(CONV_RB, D)
        sap_ref[0] = ext_ref[TM + first:TM + _A_HIST, :]
        ext_ref[0:_A_HIST, :] = ext_ref[TM:TM + _A_HIST, :]

    @pl.when(i == N_PROMPT_TILES)
    def _():
        glu_s_ref[...] = glu
        ext_ref[0:TM, :] = glu
        hist = CONV_A - 1
        for rb in range(DEC_BATCH // CONV_RB):
            r0 = rb * CONV_RB
            for l in range(DEC_SEQ):
                acc = jnp.broadcast_to(bias, (CONV_RB, D)).reshape(CONV_RB // SUBLANES, SUBLANES, D)
                for j in range(l, hist):
                    rows = st_ref[j, r0:r0 + CONV_RB, :].reshape(CONV_RB // SUBLANES, SUBLANES, D)
                    acc = acc + rows * cwb_ref[j - l][None]
                for lp in range(l + 1):
                    s = lp * DEC_BATCH + r0
                    rows = ext_ref[s:s + CONV_RB, :].reshape(CONV_RB // SUBLANES, SUBLANES, D)
                    acc = acc + rows * cwb_ref[hist + lp - l][None]
                c_ref[l * DEC_BATCH + r0:l * DEC_BATCH + r0 + CONV_RB, :] = acc.reshape(CONV_RB, D)

    c = _ln(c_ref[...], lng_ref[...], lnb_ref[...])
    c = c * jax.nn.sigmoid(c)
    m = _dot(c.astype(_BF), w_out_ref[...])
    o_ref[...] = x + _rms(m, g_ref[1:2, :])


def _mix_a(h, g2, w_in, cw, cb, lng, lnb, w_out, st_t):
    return pl.pallas_call(
        _mix_a_body,
        grid=(N_TILES,),
        in_specs=[_tile_spec(), _const_spec((2, D)), _const_spec((D, 2 * D)),
                  _const_spec((CONV_A, D)), _const_spec((1, D)), _const_spec((1, D)),
                  _const_spec((1, D)), _const_spec((D, D)),
                  _const_spec((CONV_A - 1, DEC_BATCH, D))],
        out_specs=[_tile_spec(),
                   pl.BlockSpec((1, CONV_A - 1, D),
                                lambda i: (_prompt_tile_index(i) // TILES_PER_SEQ, 0, 0)),
                   pl.BlockSpec((TM, D), lambda i: (0, 0))],
        out_shape=[jax.ShapeDtypeStruct((ROWS, D), _F32),
                   jax.ShapeDtypeStruct((BATCH, CONV_A - 1, D), _F32),
                   jax.ShapeDtypeStruct((TM, D), _F32)],
        scratch_shapes=[pltpu.VMEM((_A_HIST + TM, D), _F32),
                        pltpu.VMEM((CONV_A, SUBLANES, D), _F32),
                        pltpu.VMEM((TM, D), _F32)],
        compiler_params=_params(),
        name="mix_a",
    )(h, g2, w_in, cw, cb, lng, lnb, w_out, st_t)


_B_HIST = 8


def _mix_b_body(h_ref, g_ref, w_in_ref, cw_ref, w_out_ref, st_ref,
                o_ref, sbp_ref, z_s_ref, ext_ref, c_ref):
    i = pl.program_id(0)
    x = h_ref[...]
    xm = _rms(x, g_ref[0:1, :]).astype(_BF)
    p = _dot(xm, w_in_ref[...])
    b_gate = p[:, :D]
    z = p[:, D:2 * D] * p[:, 2 * D:]
    w0 = cw_ref[0:1, :]
    w1 = cw_ref[1:2, :]
    w2 = cw_ref[2:3, :]

    @pl.when(i < N_PROMPT_TILES)
    def _():
        @pl.when(i % TILES_PER_SEQ == 0)
        def _():
            ext_ref[0:_B_HIST, :] = jnp.zeros((_B_HIST, D), _F32)

        ext_ref[_B_HIST:_B_HIST + TM, :] = z
        c_ref[...] = (ext_ref[_B_HIST - 2:_B_HIST - 2 + TM, :] * w0
                      + ext_ref[_B_HIST - 1:_B_HIST - 1 + TM, :] * w1
                      + z * w2)
        sbp_ref[0] = z[TM - (CONV_B - 1):, :]
        ext_ref[0:_B_HIST, :] = ext_ref[TM:TM + _B_HIST, :]

    @pl.when(i == N_PROMPT_TILES)
    def _():
        zs = [st_ref[0], st_ref[1]] + [z[l * DEC_BATCH:(l + 1) * DEC_BATCH, :] for l in range(DEC_SEQ)]
        for l in range(DEC_SEQ):
            c_ref[l * DEC_BATCH:(l + 1) * DEC_BATCH, :] = zs[l] * w0 + zs[l + 1] * w1 + zs[l + 2] * w2
        z_s_ref[...] = z[(DEC_SEQ - (CONV_B - 1)) * DEC_BATCH:, :]

    m = _dot((b_gate * c_ref[...]).astype(_BF), w_out_ref[...])
    o_ref[...] = x + _rms(m, g_ref[1:2, :])


def _mix_b(h, g2, w_in, cw, w_out, st_t):
    n_keep = CONV_B - 1
    return pl.pallas_call(
        _mix_b_body,
        grid=(N_TILES,),
        in_specs=[_tile_spec(), _const_spec((2, D)), _const_spec((D, 3 * D)),
                  _const_spec((CONV_B, D)), _const_spec((D, D)),
                  _const_spec((n_keep, DEC_BATCH, D))],
        out_specs=[_tile_spec(),
                   pl.BlockSpec((1, n_keep, D),
                                lambda i: (_prompt_tile_index(i) // TILES_PER_SEQ, 0, 0)),
                   pl.BlockSpec((n_keep * DEC_BATCH, D), lambda i: (0, 0))],
        out_shape=[jax.ShapeDtypeStruct((ROWS, D), _F32),
                   jax.ShapeDtypeStruct((BATCH, n_keep, D), _F32),
                   jax.ShapeDtypeStruct((n_keep * DEC_BATCH, D), _F32)],
        scratch_shapes=[pltpu.VMEM((_B_HIST + TM, D), _F32),
                        pltpu.VMEM((TM, D), _F32)],
        compiler_params=_params(),
        name="mix_b",
    )(h, g2, w_in, cw, w_out, st_t)


_HD = D // HEADS


def _mix_c_body(h_ref, g_ref, w_in_ref, lng_ref, lnb_ref, ws_ref, bst_ref, wsv_ref, bsv_ref, w_out_ref,
                o_ref, v_s_ref, s_ref):
    i = pl.program_id(0)
    x = h_ref[...]
    xm = _rms(x, g_ref[0:1, :]).astype(_BF)
    zz = _gelu(_dot(xm, w_in_ref[...]))
    u = zz[:, :D]
    v = _ln(zz[:, D:], lng_ref[...], lnb_ref[...])

    @pl.when(i < N_PROMPT_TILES)
    def _():
        vb = v.astype(_BF)
        n_chunks = TM // CHUNK
        row = lax.broadcasted_iota(jnp.int32, (CHUNK, CHUNK), 0)
        col = lax.broadcasted_iota(jnp.int32, (CHUNK, CHUNK), 1)
        causal = col <= row
        for hd in range(HEADS):
            lanes = slice(hd * _HD, (hd + 1) * _HD)
            wm = jnp.where(causal, ws_ref[hd], 0.0).astype(_BF)
            rhs = jnp.concatenate([vb[n * CHUNK:(n + 1) * CHUNK, lanes] for n in range(n_chunks)], axis=1)
            out = _dot(wm, rhs) + bst_ref[:, hd:hd + 1]
            for n in range(n_chunks):
                s_ref[n * CHUNK:(n + 1) * CHUNK, lanes] = out[:, n * CHUNK:(n + 1) * CHUNK]

    @pl.when(i == N_PROMPT_TILES)
    def _():
        v_s_ref[...] = v
        for l in range(DEC_SEQ):
            acc = jnp.broadcast_to(bsv_ref[l:l + 1, :], (DEC_BATCH, D))
            for lp in range(l + 1):
                acc = acc + v[lp * DEC_BATCH:(lp + 1) * DEC_BATCH, :] * wsv_ref[l, lp:lp + 1, :]
            s_ref[l * DEC_BATCH:(l + 1) * DEC_BATCH, :] = acc

    m = _dot((u * s_ref[...]).astype(_BF), w_out_ref[...])
    o_ref[...] = x + _rms(m, g_ref[1:2, :])


def _mix_c(h, g2, w_in, lng, lnb, ws, bst, wsv, bsv, w_out):
    return pl.pallas_call(
        _mix_c_body,
        grid=(N_TILES,),
        in_specs=[_tile_spec(), _const_spec((2, D)), _const_spec((D, 2 * D)),
                  _const_spec((1, D)), _const_spec((1, D)),
                  _const_spec((HEADS, CHUNK, CHUNK)), _const_spec((CHUNK, HEADS)),
                  _const_spec((DEC_SEQ, DEC_SEQ, D)), _const_spec((DEC_SEQ, D)),
                  _const_spec((D, D))],
        out_specs=[_tile_spec(), pl.BlockSpec((TM, D), lambda i: (0, 0))],
        out_shape=[jax.ShapeDtypeStruct((ROWS, D), _F32),
                   jax.ShapeDtypeStruct((TM, D), _F32)],
        scratch_shapes=[pltpu.VMEM((TM, D), _F32)],
        compiler_params=_params(),
        name="mix_c",
    )(h, g2, w_in, lng, lnb, ws, bst, wsv, bsv, w_out)


_D_HIST = 16


def _mix_d_body(h_ref, g_ref, wg_ref, scale_ref, st_ref,
                o_ref, spp_ref, xm_s_ref, ext_ref, y_ref):
    i = pl.program_id(0)
    x = h_ref[...]
    xm = _rms(x, g_ref[0:1, :])

    @pl.when(i < N_PROMPT_TILES)
    def _():
        @pl.when(i % TILES_PER_SEQ == 0)
        def _():
            ext_ref[0:_D_HIST, :] = jnp.zeros((_D_HIST, D), _F32)

        ext_ref[_D_HIST:_D_HIST + TM, :] = xm
        pos = (i % TILES_PER_SEQ) * TM + lax.broadcasted_iota(jnp.int32, (TM, 1), 0)
        for gi, w in enumerate(POOL_WINDOWS):
            lanes = slice(gi * POOL_GD, (gi + 1) * POOL_GD)
            tot = xm[:, lanes]
            for j in range(1, w):
                tot = tot + ext_ref[_D_HIST - j:_D_HIST - j + TM, lanes]
            cnt = jnp.minimum(w, pos + 1).astype(_F32)
            diff = tot / cnt - xm[:, lanes]
            y_ref[:, lanes] = _dot(diff.astype(_BF), wg_ref[gi])
        spp_ref[0] = ext_ref[_D_HIST + TM - POOL_PAD:_D_HIST + TM, :]
        ext_ref[0:_D_HIST, :] = ext_ref[TM:TM + _D_HIST, :]

    @pl.when(i == N_PROMPT_TILES)
    def _():
        xm_s_ref[...] = xm
        for gi, w in enumerate(POOL_WINDOWS):
            lanes = slice(gi * POOL_GD, (gi + 1) * POOL_GD)
            for l in range(DEC_SEQ):
                cur = xm[l * DEC_BATCH:(l + 1) * DEC_BATCH, lanes]
                tot = cur
                for j in range(1, w):
                    lp = l - j
                    if lp >= 0:
                        tot = tot + xm[lp * DEC_BATCH:(lp + 1) * DEC_BATCH, lanes]
                    else:
                        tot = tot + st_ref[POOL_PAD + lp, :, lanes]
                diff = tot / float(w) - cur
                y_ref[l * DEC_BATCH:(l + 1) * DEC_BATCH, lanes] = _dot(diff.astype(_BF), wg_ref[gi])

    m = y_ref[...] * scale_ref[...]
    o_ref[...] = x + _rms(m, g_ref[1:2, :])


def _mix_d(h, g2, wg, scale, st_t):
    return pl.pallas_call(
        _mix_d_body,
        grid=(N_TILES,),
        in_specs=[_tile_spec(), _const_spec((2, D)),
                  _const_spec((len(POOL_WINDOWS), POOL_GD, POOL_GD)), _const_spec((1, D)),
                  _const_spec((POOL_PAD, DEC_BATCH, D))],
        out_specs=[_tile_spec(),
                   pl.BlockSpec((1, POOL_PAD, D),
                                lambda i: (_prompt_tile_index(i) // TILES_PER_SEQ, 0, 0)),
                   pl.BlockSpec((TM, D), lambda i: (0, 0))],
        out_shape=[jax.ShapeDtypeStruct((ROWS, D), _F32),
                   jax.ShapeDtypeStruct((BATCH, POOL_PAD, D), _F32),
                   jax.ShapeDtypeStruct((TM, D), _F32)],
        scratch_shapes=[pltpu.VMEM((_D_HIST + TM, D), _F32),
                        pltpu.VMEM((TM, D), _F32)],
        compiler_params=_params(),
        name="mix_d",
    )(h, g2, wg, scale, st_t)


def _to_pos_major(a):
    return jnp.transpose(a, (1, 0, 2))


def _from_pos_major(a, length):
    return jnp.transpose(a.reshape(length, DEC_BATCH, D), (1, 0, 2))


def kernel(x_prompt, x_sample, state_conv_a, state_conv_b, state_pool, norm_g, ffn_w_in, ffn_w_down,
           a_w_in, a_conv_w, a_conv_b, a_ln_g, a_ln_b, a_w_out, b_w_in, b_conv_w, b_w_out,
           c_w_in, c_ln_g, c_ln_b, c_ws, c_bs, c_w_out, d_w_group, d_scale):
    row = lambda a: a.reshape(1, D)
    w_in = ffn_w_in.astype(_BF)
    w_down = ffn_w_down.astype(_BF)

    xp = x_prompt.reshape(BATCH * SEQ, D)
    xs = _to_pos_major(x_sample).reshape(TM, D)

    h = _ffn_first(xp, xs, norm_g[0, 0:2], w_in, w_down)
    h, sa_p, glu_s = _mix_a(h, norm_g[0, 2:4], a_w_in.astype(_BF), a_conv_w, row(a_conv_b),
                            row(a_ln_g), row(a_ln_b), a_w_out.astype(_BF), _to_pos_major(state_conv_a))
    h = _ffn(h, norm_g[0, 4:6], w_in, w_down, 0, 1)

    h = _ffn(h, norm_g[1, 0:2], w_in, w_down, 1, 0)
    h, sb_p, z_s = _mix_b(h, norm_g[1, 2:4], b_w_in.astype(_BF), b_conv_w, b_w_out.astype(_BF),
                          _to_pos_major(state_conv_b))
    h = _ffn(h, norm_g[1, 4:6], w_in, w_down, 1, 1)

    h = _ffn(h, norm_g[2, 0:2], w_in, w_down, 2, 0)
    wsv = jnp.repeat(jnp.transpose(c_ws[:, :DEC_SEQ, :DEC_SEQ], (1, 2, 0)), _HD, axis=-1)
    bsv = jnp.repeat(jnp.transpose(c_bs[:, :DEC_SEQ]), _HD, axis=-1)
    h, v_s = _mix_c(h, norm_g[2, 2:4], c_w_in.astype(_BF), row(c_ln_g), row(c_ln_b),
                    c_ws, jnp.transpose(c_bs), wsv, bsv, c_w_out.astype(_BF))
    h = _ffn(h, norm_g[2, 4:6], w_in, w_down, 2, 1)

    h = _ffn(h, norm_g[3, 0:2], w_in, w_down, 3, 0)
    h, sp_p, xm_s = _mix_d(h, norm_g[3, 2:4], d_w_group.astype(_BF), row(d_scale),
                           _to_pos_major(state_pool))
    yp, ys = _ffn_last(h, norm_g[3, 4:6], w_in, w_down)

    y_prompt = yp.reshape(BATCH, SEQ, D)
    y_sample = _from_pos_major(ys, DEC_SEQ)
    sa_s = jnp.concatenate([state_conv_a[:, DEC_SEQ:], _from_pos_major(glu_s, DEC_SEQ)], axis=1)
    sb_s = _from_pos_major(z_s, CONV_B - 1)
    sc_s = _from_pos_major(v_s, DEC_SEQ)
    sp_s = jnp.concatenate([state_pool[:, DEC_SEQ:], _from_pos_major(xm_s, DEC_SEQ)], axis=1)
    return (y_prompt, y_sample, sa_p, sa_s, sb_p, sb_s, sc_s, sp_p, sp_s)
```

```python
import functools
from typing import Any, NamedTuple

import jax
import jax.numpy as jnp
from jax import lax
from jax.experimental import pallas as pl
from jax.experimental.pallas import tpu as pltpu

D = 1024
D_FF = 2816
DEPTH = 4
EPS = 1e-6
FFN_HALF = 0.5
CONV_A = 31
CONV_B = 3
CHUNK = 128
HEADS = 8
POOL_WINDOWS = (2, 4, 8, 16)
POOL_GD = D // len(POOL_WINDOWS)
POOL_PAD = 15

BATCH = 8
SEQ = 2048
DEC_BATCH = 128
DEC_SEQ = 4

TM = 512
TILES_PER_SEQ = SEQ // TM
N_PROMPT_TILES = BATCH * SEQ // TM
N_TILES = N_PROMPT_TILES + 1
ROWS = N_TILES * TM
FF_CHUNK = 256
SUBLANES = 8
LANES = 128

FFN_CAST_STEPS = 22
MIX_CAST_STEPS = 8

_BF = jnp.bfloat16
_F32 = jnp.float32
_VMEM_LIMIT = 60 * 1024 * 1024


def _dot(a, b):
    return jnp.dot(a, b, preferred_element_type=_F32)


def _rms(x, g):
    return x * lax.rsqrt(jnp.mean(x * x, axis=-1, keepdims=True) + EPS) * g


def _ln(x, g, b):
    xc = x - jnp.mean(x, axis=-1, keepdims=True)
    return xc * lax.rsqrt(jnp.mean(xc * xc, axis=-1, keepdims=True) + EPS) * g + b


def _gelu(x):
    return 0.5 * x * (1.0 + lax.erf(x * (2.0 ** -0.5)))


def _const_spec(shape):
    zeros = (0,) * len(shape)
    return pl.BlockSpec(shape, lambda i: zeros, pipeline_mode=pl.Buffered(1))


def _tile_spec():
    return pl.BlockSpec((TM, D), lambda i: (i, 0))


def _prompt_tile_index(i):
    return jnp.minimum(i, N_PROMPT_TILES - 1)


def _params():
    return pltpu.CompilerParams(dimension_semantics=("arbitrary",),
                                vmem_limit_bytes=_VMEM_LIMIT)


class _Cast(NamedTuple):
    src: Any
    in_spec: Any
    out_spec: Any
    out_shape: Any


def _ffn_weight_casts(ffn_w_in, ffn_w_down, layer, half):
    n = FFN_CAST_STEPS
    cb = 2 * D_FF // n
    rb = D_FF // n
    step = lambda i: jnp.minimum(i, n - 1)
    return [
        _Cast(ffn_w_in,
              pl.BlockSpec((None, None, D, cb), lambda i: (layer, half, 0, step(i))),
              pl.BlockSpec((D, cb), lambda i: (0, step(i))),
              jax.ShapeDtypeStruct((D, 2 * D_FF), _BF)),
        _Cast(ffn_w_down,
              pl.BlockSpec((None, None, rb, D), lambda i: (layer, half, step(i), 0)),
              pl.BlockSpec((rb, D), lambda i: (step(i), 0)),
              jax.ShapeDtypeStruct((D_FF, D), _BF)),
    ]


def _matrix_cast(w):
    n = MIX_CAST_STEPS
    rows, cols = w.shape
    rb = rows // n
    step = lambda i: jnp.minimum(i, n - 1)
    return _Cast(w,
                 pl.BlockSpec((rb, cols), lambda i: (step(i), 0)),
                 pl.BlockSpec((rb, cols), lambda i: (step(i), 0)),
                 jax.ShapeDtypeStruct((rows, cols), _BF))


def _run_casts(in_refs, out_refs):
    for src_ref, dst_ref in zip(in_refs, out_refs):
        dst_ref[...] = src_ref[...].astype(_BF)


def _ffn_math(x, g_ref, w_in_ref, w_down_ref):
    xn = _rms(x, g_ref[0:1, :]).astype(_BF)
    acc = None
    for c in range(D_FF // FF_CHUNK):
        c0 = c * FF_CHUNK
        gate = _dot(xn, w_in_ref[:, c0:c0 + FF_CHUNK])
        up = _dot(xn, w_in_ref[:, D_FF + c0:D_FF + c0 + FF_CHUNK])
        act = (gate * jax.nn.sigmoid(gate) * up).astype(_BF)
        d = _dot(act, w_down_ref[c0:c0 + FF_CHUNK, :])
        acc = d if acc is None else acc + d
    return x + FFN_HALF * _rms(acc, g_ref[1:2, :])


def _ffn_body(*refs, split_in, split_out, n_cast):
    refs = list(refs)
    n_x = 2 if split_in else 1
    x_refs = refs[:n_x]
    g_ref, w_in_ref, w_down_ref = refs[n_x:n_x + 3]
    cast_in = refs[n_x + 3:n_x + 3 + n_cast]
    outs = refs[n_x + 3 + n_cast:]
    n_y = 2 if split_out else 1
    y_refs = outs[:n_y]
    cast_out = outs[n_y:]

    i = pl.program_id(0)
    if split_in:
        x = jnp.where(i < N_PROMPT_TILES, x_refs[0][...], x_refs[1][...])
    else:
        x = x_refs[0][...]
    y = _ffn_math(x, g_ref, w_in_ref, w_down_ref)
    if split_out:
        @pl.when(i < N_PROMPT_TILES)
        def _():
            y_refs[0][...] = y

        @pl.when(i == N_PROMPT_TILES)
        def _():
            y_refs[1][...] = y
    else:
        y_refs[0][...] = y
    _run_casts(cast_in, cast_out)


def _ffn(xs, g2, w_in, w_down, casts, *, split_in=False, split_out=False, name):
    split_specs = [pl.BlockSpec((TM, D), lambda i: (_prompt_tile_index(i), 0)),
                   pl.BlockSpec((TM, D), lambda i: (0, 0))]
    split_shapes = [jax.ShapeDtypeStruct((N_PROMPT_TILES * TM, D), _F32),
                    jax.ShapeDtypeStruct((TM, D), _F32)]
    x_specs = split_specs if split_in else [_tile_spec()]
    y_specs = split_specs if split_out else [_tile_spec()]
    y_shapes = split_shapes if split_out else [jax.ShapeDtypeStruct((ROWS, D), _F32)]
    body = functools.partial(_ffn_body, split_in=split_in, split_out=split_out,
                             n_cast=len(casts))
    outs = pl.pallas_call(
        body,
        grid=(N_TILES,),
        in_specs=x_specs + [_const_spec((2, D)), _const_spec((D, 2 * D_FF)), _const_spec((D_FF, D))]
        + [c.in_spec for c in casts],
        out_specs=y_specs + [c.out_spec for c in casts],
        out_shape=y_shapes + [c.out_shape for c in casts],
        compiler_params=_params(),
        name=name,
    )(*xs, g2, w_in, w_down, *[c.src for c in casts])
    n_y = len(y_specs)
    return outs[:n_y], outs[n_y:]


_HIST_BB = 32


def _hist_body(sta_ref, cw_ref, stp_ref, ha_ref, hp_ref):
    for l in range(DEC_SEQ):
        acc = jnp.zeros((_HIST_BB, D), _F32)
        for j in range(l, CONV_A - 1):
            acc = acc + sta_ref[j] * cw_ref[j - l:j - l + 1, :]
        ha_ref[l] = acc
        for gi, w in enumerate(POOL_WINDOWS):
            lanes = slice(gi * POOL_GD, (gi + 1) * POOL_GD)
            tot = jnp.zeros((_HIST_BB, POOL_GD), _F32)
            for j in range(l + 1, w):
                tot = tot + stp_ref[POOL_PAD + l - j, :, lanes]
            hp_ref[l, :, lanes] = tot


def _hist(st_a_t, cw, st_p_t):
    out = jax.ShapeDtypeStruct((DEC_SEQ, DEC_BATCH, D), _F32)
    ha, hp = pl.pallas_call(
        _hist_body,
        grid=(DEC_BATCH // _HIST_BB,),
        in_specs=[pl.BlockSpec((CONV_A - 1, _HIST_BB, D), lambda i: (0, i, 0)),
                  pl.BlockSpec((CONV_A, D), lambda i: (0, 0)),
                  pl.BlockSpec((POOL_PAD, _HIST_BB, D), lambda i: (0, i, 0))],
        out_specs=[pl.BlockSpec((DEC_SEQ, _HIST_BB, D), lambda i: (0, i, 0)),
                   pl.BlockSpec((DEC_SEQ, _HIST_BB, D), lambda i: (0, i, 0))],
        out_shape=[out, out],
        compiler_params=_params(),
        name="sample_hist",
    )(st_a_t, cw, st_p_t)
    return ha.reshape(TM, D), hp.reshape(TM, D)


_A_HIST = 32
_A_FIRST = _A_HIST - (CONV_A - 1)
_A_LB = 256
_A_RB = 64
_A_SHIFT_ROWS = 536


def _mix_a_body(h_ref, g_ref, w_in_ref, cw_ref, cb_ref, lng_ref, lnb_ref, w_out_ref, ha_ref,
                o_ref, sap_ref, glu_s_ref, ext_ref, cwb_ref, c_ref, sh_ref):
    i = pl.program_id(0)

    @pl.when(i == 0)
    def _():
        for k in range(CONV_A):
            cwb_ref[k] = jnp.broadcast_to(cw_ref[k:k + 1, :], (SUBLANES, D))

    x = h_ref[...]
    xm = _rms(x, g_ref[0:1, :]).astype(_BF)
    a = _dot(xm, w_in_ref[...])
    glu = a[:, :D] * jax.nn.sigmoid(a[:, D:])

    def blocks(v, rows, lanes):
        return v.reshape(rows // SUBLANES, SUBLANES, lanes)

    @pl.when(i < N_PROMPT_TILES)
    def _():
        @pl.when(i % TILES_PER_SEQ == 0)
        def _():
            ext_ref[0:_A_HIST, :] = jnp.zeros((_A_HIST, D), _F32)

        ext_ref[_A_HIST:_A_HIST + TM, :] = glu
        for lb in range(D // _A_LB):
            lanes = slice(lb * _A_LB, (lb + 1) * _A_LB)
            for r in range(1, SUBLANES):
                sh_ref[r - 1] = ext_ref[r:r + _A_SHIFT_ROWS, lanes]
            bias = jnp.broadcast_to(cb_ref[:, lanes], (SUBLANES, _A_LB))[None]
            for rb in range(TM // _A_RB):
                r0 = rb * _A_RB
                acc = jnp.broadcast_to(bias, (_A_RB // SUBLANES, SUBLANES, _A_LB))
                for k in range(CONV_A):
                    r = (_A_FIRST + k) % SUBLANES
                    s = r0 + _A_FIRST + k - r
                    if r == 0:
                        rows = ext_ref[s:s + _A_RB, lanes]
                    else:
                        rows = sh_ref[r - 1, s:s + _A_RB, :]
                    acc = acc + blocks(rows, _A_RB, _A_LB) * cwb_ref[k][:, lanes][None]
                c_ref[r0:r0 + _A_RB, lanes] = acc.reshape(_A_RB, _A_LB)
        sap_ref[0] = ext_ref[TM + _A_FIRST:TM + _A_HIST, :]
        ext_ref[0:_A_HIST, :] = ext_ref[TM:TM + _A_HIST, :]

    @pl.when(i == N_PROMPT_TILES)
    def _():
        glu_s_ref[...] = glu
        ext_ref[0:TM, :] = glu
        bias = jnp.broadcast_to(cb_ref[...], (SUBLANES, D))[None]
        for l in range(DEC_SEQ):
            for rb in range(DEC_BATCH // _A_RB):
                r0 = l * DEC_BATCH + rb * _A_RB
                acc = blocks(ha_ref[r0:r0 + _A_RB, :], _A_RB, D) + bias
                for lp in range(l + 1):
                    s = lp * DEC_BATCH + rb * _A_RB
                    acc = acc + blocks(ext_ref[s:s + _A_RB, :], _A_RB, D) * cwb_ref[CONV_A - 1 + lp - l][None]
                c_ref[r0:r0 + _A_RB, :] = acc.reshape(_A_RB, D)

    c = _ln(c_ref[...], lng_ref[...], lnb_ref[...])
    c = c * jax.nn.sigmoid(c)
    m = _dot(c.astype(_BF), w_out_ref[...])
    o_ref[...] = x + _rms(m, g_ref[1:2, :])


def _mix_a(h, g2, w_in, cw, cb, lng, lnb, w_out, ha):
    return pl.pallas_call(
        _mix_a_body,
        grid=(N_TILES,),
        in_specs=[_tile_spec(), _const_spec((2, D)), _const_spec((D, 2 * D)),
                  _const_spec((CONV_A, D)), _const_spec((1, D)), _const_spec((1, D)),
                  _const_spec((1, D)), _const_spec((D, D)), _const_spec((TM, D))],
        out_specs=[_tile_spec(),
                   pl.BlockSpec((1, CONV_A - 1, D),
                                lambda i: (_prompt_tile_index(i) // TILES_PER_SEQ, 0, 0)),
                   pl.BlockSpec((TM, D), lambda i: (0, 0))],
        out_shape=[jax.ShapeDtypeStruct((ROWS, D), _F32),
                   jax.ShapeDtypeStruct((BATCH, CONV_A - 1, D), _F32),
                   jax.ShapeDtypeStruct((TM, D), _F32)],
        scratch_shapes=[pltpu.VMEM((_A_HIST + TM, D), _F32),
                        pltpu.VMEM((CONV_A, SUBLANES, D), _F32),
                        pltpu.VMEM((TM, D), _F32),
                        pltpu.VMEM((SUBLANES - 1, _A_SHIFT_ROWS, _A_LB), _F32)],
        compiler_params=_params(),
        name="mix_a",
    )(h, g2, w_in, cw, cb, lng, lnb, w_out, ha)


_B_HIST = 8


def _mix_b_body(h_ref, g_ref, w_in_ref, cw_ref, w_out_ref, st_ref,
                o_ref, sbp_ref, z_s_ref, ext_ref, c_ref):
    i = pl.program_id(0)
    x = h_ref[...]
    xm = _rms(x, g_ref[0:1, :]).astype(_BF)
    p = _dot(xm, w_in_ref[...])
    b_gate = p[:, :D]
    z = p[:, D:2 * D] * p[:, 2 * D:]
    w0 = cw_ref[0:1, :]
    w1 = cw_ref[1:2, :]
    w2 = cw_ref[2:3, :]

    @pl.when(i < N_PROMPT_TILES)
    def _():
        @pl.when(i % TILES_PER_SEQ == 0)
        def _():
            ext_ref[0:_B_HIST, :] = jnp.zeros((_B_HIST, D), _F32)

        ext_ref[_B_HIST:_B_HIST + TM, :] = z
        c_ref[...] = (ext_ref[_B_HIST - 2:_B_HIST - 2 + TM, :] * w0
                      + ext_ref[_B_HIST - 1:_B_HIST - 1 + TM, :] * w1
                      + z * w2)
        sbp_ref[0] = z[TM - (CONV_B - 1):, :]
        ext_ref[0:_B_HIST, :] = ext_ref[TM:TM + _B_HIST, :]

    @pl.when(i == N_PROMPT_TILES)
    def _():
        zs = [st_ref[0], st_ref[1]] + [z[l * DEC_BATCH:(l + 1) * DEC_BATCH, :] for l in range(DEC_SEQ)]
        for l in range(DEC_SEQ):
            c_ref[l * DEC_BATCH:(l + 1) * DEC_BATCH, :] = zs[l] * w0 + zs[l + 1] * w1 + zs[l + 2] * w2
        z_s_ref[...] = z[(DEC_SEQ - (CONV_B - 1)) * DEC_BATCH:, :]

    m = _dot((b_gate * c_ref[...]).astype(_BF), w_out_ref[...])
    o_ref[...] = x + _rms(m, g_ref[1:2, :])


def _mix_b(h, g2, w_in, cw, w_out, st_t):
    n_keep = CONV_B - 1
    return pl.pallas_call(
        _mix_b_body,
        grid=(N_TILES,),
        in_specs=[_tile_spec(), _const_spec((2, D)), _const_spec((D, 3 * D)),
                  _const_spec((CONV_B, D)), _const_spec((D, D)),
                  _const_spec((n_keep, DEC_BATCH, D))],
        out_specs=[_tile_spec(),
                   pl.BlockSpec((1, n_keep, D),
                                lambda i: (_prompt_tile_index(i) // TILES_PER_SEQ, 0, 0)),
                   pl.BlockSpec((n_keep * DEC_BATCH, D), lambda i: (0, 0))],
        out_shape=[jax.ShapeDtypeStruct((ROWS, D), _F32),
                   jax.ShapeDtypeStruct((BATCH, n_keep, D), _F32),
                   jax.ShapeDtypeStruct((n_keep * DEC_BATCH, D), _F32)],
        scratch_shapes=[pltpu.VMEM((_B_HIST + TM, D), _F32),
                        pltpu.VMEM((TM, D), _F32)],
        compiler_params=_params(),
        name="mix_b",
    )(h, g2, w_in, cw, w_out, st_t)


_HD = D // HEADS


def _mix_c_body(h_ref, g_ref, w_in_ref, lng_ref, lnb_ref, ws_ref, bst_ref, wsv_ref, bsv_ref, w_out_ref,
                o_ref, v_s_ref, s_ref):
    i = pl.program_id(0)
    x = h_ref[...]
    xm = _rms(x, g_ref[0:1, :]).astype(_BF)
    zz = _gelu(_dot(xm, w_in_ref[...]))
    u = zz[:, :D]
    v = _ln(zz[:, D:], lng_ref[...], lnb_ref[...])

    @pl.when(i < N_PROMPT_TILES)
    def _():
        vb = v.astype(_BF)
        n_chunks = TM // CHUNK
        row = lax.broadcasted_iota(jnp.int32, (CHUNK, CHUNK), 0)
        col = lax.broadcasted_iota(jnp.int32, (CHUNK, CHUNK), 1)
        causal = col <= row
        for hd in range(HEADS):
            lanes = slice(hd * _HD, (hd + 1) * _HD)
            wm = jnp.where(causal, ws_ref[hd], 0.0).astype(_BF)
            rhs = jnp.concatenate([vb[n * CHUNK:(n + 1) * CHUNK, lanes] for n in range(n_chunks)], axis=1)
            out = _dot(wm, rhs) + bst_ref[:, hd:hd + 1]
            for n in range(n_chunks):
                s_ref[n * CHUNK:(n + 1) * CHUNK, lanes] = out[:, n * CHUNK:(n + 1) * CHUNK]

    @pl.when(i == N_PROMPT_TILES)
    def _():
        v_s_ref[...] = v
        for l in range(DEC_SEQ):
            acc = jnp.broadcast_to(bsv_ref[l:l + 1, :], (DEC_BATCH, D))
            for lp in range(l + 1):
                acc = acc + v[lp * DEC_BATCH:(lp + 1) * DEC_BATCH, :] * wsv_ref[l, lp:lp + 1, :]
            s_ref[l * DEC_BATCH:(l + 1) * DEC_BATCH, :] = acc

    m = _dot((u * s_ref[...]).astype(_BF), w_out_ref[...])
    o_ref[...] = x + _rms(m, g_ref[1:2, :])


def _mix_c(h, g2, w_in, lng, lnb, ws, bst, wsv, bsv, w_out):
    return pl.pallas_call(
        _mix_c_body,
        grid=(N_TILES,),
        in_specs=[_tile_spec(), _const_spec((2, D)), _const_spec((D, 2 * D)),
                  _const_spec((1, D)), _const_spec((1, D)),
                  _const_spec((HEADS, CHUNK, CHUNK)), _const_spec((CHUNK, HEADS)),
                  _const_spec((DEC_SEQ, DEC_SEQ, D)), _const_spec((DEC_SEQ, D)),
                  _const_spec((D, D))],
        out_specs=[_tile_spec(), pl.BlockSpec((TM, D), lambda i: (0, 0))],
        out_shape=[jax.ShapeDtypeStruct((ROWS, D), _F32),
                   jax.ShapeDtypeStruct((TM, D), _F32)],
        scratch_shapes=[pltpu.VMEM((TM, D), _F32)],
        compiler_params=_params(),
        name="mix_c",
    )(h, g2, w_in, lng, lnb, ws, bst, wsv, bsv, w_out)


_D_HIST = 16


def _mix_d_body(h_ref, g_ref, wg_ref, scale_ref, hp_ref,
                o_ref, spp_ref, xm_s_ref, ext_ref, y_ref):
    i = pl.program_id(0)
    x = h_ref[...]
    xm = _rms(x, g_ref[0:1, :])

    @pl.when(i < N_PROMPT_TILES)
    def _():
        @pl.when(i % TILES_PER_SEQ == 0)
        def _():
            ext_ref[0:_D_HIST, :] = jnp.zeros((_D_HIST, D), _F32)

        ext_ref[_D_HIST:_D_HIST + TM, :] = xm
        pos = (i % TILES_PER_SEQ) * TM + lax.broadcasted_iota(jnp.int32, (TM, 1), 0)
        for gi, w in enumerate(POOL_WINDOWS):
            lanes = slice(gi * POOL_GD, (gi + 1) * POOL_GD)
            tot = xm[:, lanes]
            for j in range(1, w):
                tot = tot + ext_ref[_D_HIST - j:_D_HIST - j + TM, lanes]
            cnt = jnp.minimum(w, pos + 1).astype(_F32)
            diff = tot / cnt - xm[:, lanes]
            y_ref[:, lanes] = _dot(diff.astype(_BF), wg_ref[gi * POOL_GD:(gi + 1) * POOL_GD, :])
        spp_ref[0] = ext_ref[_D_HIST + TM - POOL_PAD:_D_HIST + TM, :]
        ext_ref[0:_D_HIST, :] = ext_ref[TM:TM + _D_HIST, :]

    @pl.when(i == N_PROMPT_TILES)
    def _():
        xm_s_ref[...] = xm
        for gi, w in enumerate(POOL_WINDOWS):
            lanes = slice(gi * POOL_GD, (gi + 1) * POOL_GD)
            for l in range(DEC_SEQ):
                rows = slice(l * DEC_BATCH, (l + 1) * DEC_BATCH)
                cur = xm[rows, lanes]
                tot = cur + hp_ref[rows, lanes]
                for j in range(1, min(w - 1, l) + 1):
                    tot = tot + xm[(l - j) * DEC_BATCH:(l - j + 1) * DEC_BATCH, lanes]
                diff = tot / float(w) - cur
                y_ref[rows, lanes] = _dot(diff.astype(_BF), wg_ref[gi * POOL_GD:(gi + 1) * POOL_GD, :])

    m = y_ref[...] * scale_ref[...]
    o_ref[...] = x + _rms(m, g_ref[1:2, :])


def _mix_d(h, g2, wg, scale, hp):
    return pl.pallas_call(
        _mix_d_body,
        grid=(N_TILES,),
        in_specs=[_tile_spec(), _const_spec((2, D)),
                  _const_spec((len(POOL_WINDOWS) * POOL_GD, POOL_GD)), _const_spec((1, D)),
                  _const_spec((TM, D))],
        out_specs=[_tile_spec(),
                   pl.BlockSpec((1, POOL_PAD, D),
                                lambda i: (_prompt_tile_index(i) // TILES_PER_SEQ, 0, 0)),
                   pl.BlockSpec((TM, D), lambda i: (0, 0))],
        out_shape=[jax.ShapeDtypeStruct((ROWS, D), _F32),
                   jax.ShapeDtypeStruct((BATCH, POOL_PAD, D), _F32),
                   jax.ShapeDtypeStruct((TM, D), _F32)],
        scratch_shapes=[pltpu.VMEM((_D_HIST + TM, D), _F32),
                        pltpu.VMEM((TM, D), _F32)],
        compiler_params=_params(),
        name="mix_d",
    )(h, g2, wg, scale, hp)


def _to_pos_major(a):
    return jnp.transpose(a, (1, 0, 2))


def _from_pos_major(a, length):
    return jnp.transpose(a.reshape(length, DEC_BATCH, D), (1, 0, 2))


def kernel(x_prompt, x_sample, state_conv_a, state_conv_b, state_pool, norm_g, ffn_w_in, ffn_w_down,
           a_w_in, a_conv_w, a_conv_b, a_ln_g, a_ln_b, a_w_out, b_w_in, b_conv_w, b_w_out,
           c_w_in, c_ln_g, c_ln_b, c_ws, c_bs, c_w_out, d_w_group, d_scale):
    row = lambda a: a.reshape(1, D)
    ffn_casts = lambda layer, half: _ffn_weight_casts(ffn_w_in, ffn_w_down, layer, half)

    xp = x_prompt.reshape(BATCH * SEQ, D)
    xs = _to_pos_major(x_sample).reshape(TM, D)
    ha, hp = _hist(_to_pos_major(state_conv_a), a_conv_w, _to_pos_major(state_pool))

    (h,), (w_in, w_down, a_in, a_out) = _ffn(
        (xp, xs), norm_g[0, 0:2], ffn_w_in[0, 0].astype(_BF), ffn_w_down[0, 0].astype(_BF),
        ffn_casts(0, 1) + [_matrix_cast(a_w_in), _matrix_cast(a_w_out)], split_in=True, name="ffn_0_0")
    h, sa_p, glu_s = _mix_a(h, norm_g[0, 2:4], a_in, a_conv_w, row(a_conv_b), row(a_ln_g), row(a_ln_b),
                            a_out, ha)
    (h,), (w_in, w_down) = _ffn((h,), norm_g[0, 4:6], w_in, w_down, ffn_casts(1, 0), name="ffn_0_1")

    (h,), (w_in, w_down, b_in, b_out) = _ffn(
        (h,), norm_g[1, 0:2], w_in, w_down,
        ffn_casts(1, 1) + [_matrix_cast(b_w_in), _matrix_cast(b_w_out)], name="ffn_1_0")
    h, sb_p, z_s = _mix_b(h, norm_g[1, 2:4], b_in, b_conv_w, b_out, _to_pos_major(state_conv_b))
    (h,), (w_in, w_down) = _ffn((h,), norm_g[1, 4:6], w_in, w_down, ffn_casts(2, 0), name="ffn_1_1")

    (h,), (w_in, w_down, c_in, c_out) = _ffn(
        (h,), norm_g[2, 0:2], w_in, w_down,
        ffn_casts(2, 1) + [_matrix_cast(c_w_in), _matrix_cast(c_w_out)], name="ffn_2_0")
    wsv = jnp.repeat(jnp.transpose(c_ws[:, :DEC_SEQ, :DEC_SEQ], (1, 2, 0)), _HD, axis=-1)
    bsv = jnp.repeat(jnp.transpose(c_bs[:, :DEC_SEQ]), _HD, axis=-1)
    h, v_s = _mix_c(h, norm_g[2, 2:4], c_in, row(c_ln_g), row(c_ln_b),
                    c_ws, jnp.transpose(c_bs), wsv, bsv, c_out)
    (h,), (w_in, w_down) = _ffn((h,), norm_g[2, 4:6], w_in, w_down, ffn_casts(3, 0), name="ffn_2_1")

    (h,), (w_in, w_down, d_wg) = _ffn(
        (h,), norm_g[3, 0:2], w_in, w_down,
        ffn_casts(3, 1) + [_matrix_cast(d_w_group.reshape(len(POOL_WINDOWS) * POOL_GD, POOL_GD))],
        name="ffn_3_0")
    h, sp_p, xm_s = _mix_d(h, norm_g[3, 2:4], d_wg, row(d_scale), hp)
    (yp, ys), _ = _ffn((h,), norm_g[3, 4:6], w_in, w_down, [], split_out=True, name="ffn_3_1")

    y_prompt = yp.reshape(BATCH, SEQ, D)
    y_sample = _from_pos_major(ys, DEC_SEQ)
    sa_s = jnp.concatenate([state_conv_a[:, DEC_SEQ:], _from_pos_major(glu_s, DEC_SEQ)], axis=1)
    sb_s = _from_pos_major(z_s, CONV_B - 1)
    sc_s = _from_pos_major(v_s, DEC_SEQ)
    sp_s = jnp.concatenate([state_pool[:, DEC_SEQ:], _from_pos_major(xm_s, DEC_SEQ)], axis=1)
    return (y_prompt, y_sample, sa_p, sa_s, sb_p, sb_s, sc_s, sp_p, sp_s)
```

```python
import functools
from typing import Any, NamedTuple

import jax
import jax.numpy as jnp
from jax import lax
from jax.experimental import pallas as pl
from jax.experimental.pallas import tpu as pltpu

D = 1024
D_FF = 2816
DEPTH = 4
EPS = 1e-6
FFN_HALF = 0.5
CONV_A = 31
CONV_B = 3
CHUNK = 128
HEADS = 8
POOL_WINDOWS = (2, 4, 8, 16)
POOL_GD = D // len(POOL_WINDOWS)
POOL_PAD = 15

BATCH = 8
SEQ = 2048
DEC_BATCH = 128
DEC_SEQ = 4

TM = 512
TILES_PER_SEQ = SEQ // TM
N_PROMPT_TILES = BATCH * SEQ // TM
N_TILES = N_PROMPT_TILES + 1
ROWS = N_TILES * TM
FF_CHUNK = 256
SUBLANES = 8
LANES = 128

FFN_CAST_STEPS = 22
MIX_CAST_STEPS = 8

_BF = jnp.bfloat16
_F32 = jnp.float32
_VMEM_LIMIT = 60 * 1024 * 1024


def _dot(a, b):
    return jnp.dot(a, b, preferred_element_type=_F32)


def _rms(x, g):
    return x * lax.rsqrt(jnp.mean(x * x, axis=-1, keepdims=True) + EPS) * g


def _ln(x, g, b):
    xc = x - jnp.mean(x, axis=-1, keepdims=True)
    return xc * lax.rsqrt(jnp.mean(xc * xc, axis=-1, keepdims=True) + EPS) * g + b


def _gelu(x):
    return 0.5 * x * (1.0 + lax.erf(x * (2.0 ** -0.5)))


def _const_spec(shape):
    zeros = (0,) * len(shape)
    return pl.BlockSpec(shape, lambda i: zeros, pipeline_mode=pl.Buffered(1))


def _tile_spec():
    return pl.BlockSpec((TM, D), lambda i: (i, 0))


def _prompt_tile_index(i):
    return jnp.minimum(i, N_PROMPT_TILES - 1)


def _params():
    return pltpu.CompilerParams(dimension_semantics=("arbitrary",),
                                vmem_limit_bytes=_VMEM_LIMIT)


class _Cast(NamedTuple):
    src: Any
    in_spec: Any
    out_spec: Any
    out_shape: Any


def _cast_block(n):
    return lambda s: jnp.clip(s - 1, 0, n - 1)


def _ffn_weight_casts(ffn_w_in, ffn_w_down, layer, half):
    n = FFN_CAST_STEPS
    cb = 2 * D_FF // n
    rb = D_FF // n
    blk = _cast_block(n)
    return [
        _Cast(ffn_w_in,
              pl.BlockSpec((None, None, D, cb), lambda s: (layer, half, 0, blk(s))),
              pl.BlockSpec((D, cb), lambda s: (0, blk(s))),
              jax.ShapeDtypeStruct((D, 2 * D_FF), _BF)),
        _Cast(ffn_w_down,
              pl.BlockSpec((None, None, rb, D), lambda s: (layer, half, blk(s), 0)),
              pl.BlockSpec((rb, D), lambda s: (blk(s), 0)),
              jax.ShapeDtypeStruct((D_FF, D), _BF)),
    ]


def _matrix_cast(w):
    n = MIX_CAST_STEPS
    rows, cols = w.shape
    rb = rows // n
    blk = _cast_block(n)
    return _Cast(w,
                 pl.BlockSpec((rb, cols), lambda s: (blk(s), 0)),
                 pl.BlockSpec((rb, cols), lambda s: (blk(s), 0)),
                 jax.ShapeDtypeStruct((rows, cols), _BF))


def _run_casts(in_refs, out_refs):
    for src_ref, dst_ref in zip(in_refs, out_refs):
        dst_ref[...] = src_ref[...].astype(_BF)


def _zero_token(v):
    t = jnp.where(v != v, v, 0.0).reshape(v.shape[0] // SUBLANES, SUBLANES, D)
    rows = t[0]
    for a in range(1, t.shape[0]):
        rows = rows + t[a]
    z = rows[:, 0:LANES]
    for k in range(1, D // LANES):
        z = z + rows[:, k * LANES:(k + 1) * LANES]
    return z


def _tie(x, token):
    rows, cols = x.shape
    z = jnp.concatenate([token] * (cols // LANES), axis=1)
    return (x.reshape(rows // SUBLANES, SUBLANES, cols) + z[None]).reshape(rows, cols)


FFN_STEPS = N_TILES + 2
_PIECES = 8
_PIECE_ROWS = TM // _PIECES


def _ffn_body(*refs, split_in, split_out, n_cast):
    refs = list(refs)
    xn_refs = refs[-5:-3]
    acc_refs = refs[-3:-1]
    act_ref = refs[-1]
    refs = refs[:-5]
    n_x = 3 if split_in else 2
    x_refs = refs[:n_x]
    g_ref, w_in_ref, w_down_ref = refs[n_x:n_x + 3]
    cast_in = refs[n_x + 3:n_x + 3 + n_cast]
    outs = refs[n_x + 3 + n_cast:]
    n_y = 2 if split_out else 1
    y_refs = outs[:n_y]
    cast_out = outs[n_y:]

    s = pl.program_id(0)
    if split_in:
        xa_ref, xs_ref, xb_ref = x_refs
        last_res_ref = xs_ref
    else:
        xa_ref, xb_ref = x_refs
        last_res_ref = xb_ref
    last_y_ref = y_refs[-1]

    def piece(k):
        return slice(k * _PIECE_ROWS, (k + 1) * _PIECE_ROWS)

    def norm_piece(par, k):
        rows = piece(k)
        x = xa_ref[rows, :]
        if split_in:
            x = jnp.where(s < N_PROMPT_TILES, x, xs_ref[rows, :])
        y = _rms(x, g_ref[0:1, :])
        xn_refs[par][rows, :] = y.astype(_BF)
        return _zero_token(y)

    def finish_piece(par, k, res_ref, y_ref):
        rows = piece(k)
        y = res_ref[rows, :] + FFN_HALF * _rms(acc_refs[par][rows, :], g_ref[1:2, :])
        y_ref[rows, :] = y
        return _zero_token(y)

    def main(par):
        token = None
        for c in range(D_FF // FF_CHUNK):
            c0 = c * FF_CHUNK
            xn = xn_refs[1 - par][...]
            gate = _dot(xn, w_in_ref[:, c0:c0 + FF_CHUNK])
            up = _dot(xn, w_in_ref[:, D_FF + c0:D_FF + c0 + FF_CHUNK])
            if token is not None:
                up = _tie(up, token)
                token = None
            act_ref[:, c0:c0 + FF_CHUNK] = (gate * jax.nn.sigmoid(gate) * up).astype(_BF)
            if c < _PIECES:
                token = finish_piece(par, c, xb_ref, y_refs[0]) + norm_piece(par, c)
        acc_refs[1 - par][...] = _dot(act_ref[...], w_down_ref[...])
        _run_casts(cast_in, cast_out)

    @pl.when(s == 0)
    def _():
        acc_refs[0][...] = jnp.zeros((TM, D), _F32)
        acc_refs[1][...] = jnp.zeros((TM, D), _F32)
        for k in range(_PIECES):
            norm_piece(0, k)

    in_main = jnp.logical_and(s >= 1, s <= N_TILES)
    for par in range(2):
        @pl.when(jnp.logical_and(in_main, s % 2 == par))
        def _(par=par):
            main(par)

    @pl.when(s == FFN_STEPS - 1)
    def _():
        for k in range(_PIECES):
            finish_piece((FFN_STEPS - 1) % 2, k, last_res_ref, last_y_ref)


def _ffn(xs, g2, w_in, w_down, casts, *, split_in=False, split_out=False, name):
    last = N_TILES - 1
    last_p = N_PROMPT_TILES - 1
    norm_tile = lambda s: (jnp.minimum(s, last), 0)
    done_tile = lambda s: (jnp.clip(s - 2, 0, last), 0)
    norm_tile_p = lambda s: (jnp.minimum(s, last_p), 0)
    done_tile_p = lambda s: (jnp.clip(s - 2, 0, last_p), 0)
    sample_tile = lambda s: (0, 0)
    if split_in:
        xp, xsm = xs
        x_args = [xp, xsm, xp]
        x_specs = [pl.BlockSpec((TM, D), norm_tile_p), pl.BlockSpec((TM, D), sample_tile),
                   pl.BlockSpec((TM, D), done_tile_p)]
    else:
        (h,) = xs
        x_args = [h, h]
        x_specs = [pl.BlockSpec((TM, D), norm_tile), pl.BlockSpec((TM, D), done_tile)]
    if split_out:
        y_specs = [pl.BlockSpec((TM, D), done_tile_p), pl.BlockSpec((TM, D), sample_tile)]
        y_shapes = [jax.ShapeDtypeStruct((N_PROMPT_TILES * TM, D), _F32),
                    jax.ShapeDtypeStruct((TM, D), _F32)]
    else:
        y_specs = [pl.BlockSpec((TM, D), done_tile)]
        y_shapes = [jax.ShapeDtypeStruct((ROWS, D), _F32)]
    body = functools.partial(_ffn_body, split_in=split_in, split_out=split_out,
                             n_cast=len(casts))
    outs = pl.pallas_call(
        body,
        grid=(FFN_STEPS,),
        in_specs=x_specs + [_const_spec((2, D)), _const_spec((D, 2 * D_FF)), _const_spec((D_FF, D))]
        + [c.in_spec for c in casts],
        out_specs=y_specs + [c.out_spec for c in casts],
        out_shape=y_shapes + [c.out_shape for c in casts],
        scratch_shapes=[pltpu.VMEM((TM, D), _BF), pltpu.VMEM((TM, D), _BF),
                        pltpu.VMEM((TM, D), _F32), pltpu.VMEM((TM, D), _F32),
                        pltpu.VMEM((TM, D_FF), _BF)],
        compiler_params=_params(),
        name=name,
    )(*x_args, g2, w_in, w_down, *[c.src for c in casts])
    n_y = len(y_specs)
    return outs[:n_y], outs[n_y:]


_HIST_BB = 32


def _hist_body(sta_ref, cw_ref, stp_ref, ha_ref, hp_ref):
    for l in range(DEC_SEQ):
        acc = jnp.zeros((_HIST_BB, D), _F32)
        for j in range(l, CONV_A - 1):
            acc = acc + sta_ref[j] * cw_ref[j - l:j - l + 1, :]
        ha_ref[l] = acc
        for gi, w in enumerate(POOL_WINDOWS):
            lanes = slice(gi * POOL_GD, (gi + 1) * POOL_GD)
            tot = jnp.zeros((_HIST_BB, POOL_GD), _F32)
            for j in range(l + 1, w):
                tot = tot + stp_ref[POOL_PAD + l - j, :, lanes]
            hp_ref[l, :, lanes] = tot


def _hist(st_a_t, cw, st_p_t):
    out = jax.ShapeDtypeStruct((DEC_SEQ, DEC_BATCH, D), _F32)
    ha, hp = pl.pallas_call(
        _hist_body,
        grid=(DEC_BATCH // _HIST_BB,),
        in_specs=[pl.BlockSpec((CONV_A - 1, _HIST_BB, D), lambda i: (0, i, 0)),
                  pl.BlockSpec((CONV_A, D), lambda i: (0, 0)),
                  pl.BlockSpec((POOL_PAD, _HIST_BB, D), lambda i: (0, i, 0))],
        out_specs=[pl.BlockSpec((DEC_SEQ, _HIST_BB, D), lambda i: (0, i, 0)),
                   pl.BlockSpec((DEC_SEQ, _HIST_BB, D), lambda i: (0, i, 0))],
        out_shape=[out, out],
        compiler_params=_params(),
        name="sample_hist",
    )(st_a_t, cw, st_p_t)
    return ha.reshape(TM, D), hp.reshape(TM, D)


_A_HIST = 32
_A_FIRST = _A_HIST - (CONV_A - 1)
_A_LB = 256
_A_RB = 64
_A_SHIFT_ROWS = 536


def _mix_a_body(h_ref, g_ref, w_in_ref, cw_ref, cb_ref, lng_ref, lnb_ref, w_out_ref, ha_ref,
                o_ref, sap_ref, glu_s_ref, ext_ref, cwb_ref, c_ref, sh_ref):
    i = pl.program_id(0)

    @pl.when(i == 0)
    def _():
        for k in range(CONV_A):
            cwb_ref[k] = jnp.broadcast_to(cw_ref[k:k + 1, :], (SUBLANES, D))

    x = h_ref[...]
    xm = _rms(x, g_ref[0:1, :]).astype(_BF)
    a = _dot(xm, w_in_ref[...])
    glu = a[:, :D] * jax.nn.sigmoid(a[:, D:])

    def blocks(v, rows, lanes):
        return v.reshape(rows // SUBLANES, SUBLANES, lanes)

    @pl.when(i < N_PROMPT_TILES)
    def _():
        @pl.when(i % TILES_PER_SEQ == 0)
        def _():
            ext_ref[0:_A_HIST, :] = jnp.zeros((_A_HIST, D), _F32)

        ext_ref[_A_HIST:_A_HIST + TM, :] = glu
        for lb in range(D // _A_LB):
            lanes = slice(lb * _A_LB, (lb + 1) * _A_LB)
            for r in range(1, SUBLANES):
                sh_ref[r - 1] = ext_ref[r:r + _A_SHIFT_ROWS, lanes]
            bias = jnp.broadcast_to(cb_ref[:, lanes], (SUBLANES, _A_LB))[None]
            for rb in range(TM // _A_RB):
                r0 = rb * _A_RB
                acc = jnp.broadcast_to(bias, (_A_RB // SUBLANES, SUBLANES, _A_LB))
                for k in range(CONV_A):
                    r = (_A_FIRST + k) % SUBLANES
                    s = r0 + _A_FIRST + k - r
                    if r == 0:
                        rows = ext_ref[s:s + _A_RB, lanes]
                    else:
                        rows = sh_ref[r - 1, s:s + _A_RB, :]
                    acc = acc + blocks(rows, _A_RB, _A_LB) * cwb_ref[k][:, lanes][None]
                c_ref[r0:r0 + _A_RB, lanes] = acc.reshape(_A_RB, _A_LB)
        sap_ref[0] = ext_ref[TM + _A_FIRST:TM + _A_HIST, :]
        ext_ref[0:_A_HIST, :] = ext_ref[TM:TM + _A_HIST, :]

    @pl.when(i == N_PROMPT_TILES)
    def _():
        glu_s_ref[...] = glu
        ext_ref[0:TM, :] = glu
        bias = jnp.broadcast_to(cb_ref[...], (SUBLANES, D))[None]
        for l in range(DEC_SEQ):
            for rb in range(DEC_BATCH // _A_RB):
                r0 = l * DEC_BATCH + rb * _A_RB
                acc = blocks(ha_ref[r0:r0 + _A_RB, :], _A_RB, D) + bias
                for lp in range(l + 1):
                    s = lp * DEC_BATCH + rb * _A_RB
                    acc = acc + blocks(ext_ref[s:s + _A_RB, :], _A_RB, D) * cwb_ref[CONV_A - 1 + lp - l][None]
                c_ref[r0:r0 + _A_RB, :] = acc.reshape(_A_RB, D)

    c = _ln(c_ref[...], lng_ref[...], lnb_ref[...])
    c = c * jax.nn.sigmoid(c)
    m = _dot(c.astype(_BF), w_out_ref[...])
    o_ref[...] = x + _rms(m, g_ref[1:2, :])


def _mix_a(h, g2, w_in, cw, cb, lng, lnb, w_out, ha):
    return pl.pallas_call(
        _mix_a_body,
        grid=(N_TILES,),
        in_specs=[_tile_spec(), _const_spec((2, D)), _const_spec((D, 2 * D)),
                  _const_spec((CONV_A, D)), _const_spec((1, D)), _const_spec((1, D)),
                  _const_spec((1, D)), _const_spec((D, D)), _const_spec((TM, D))],
        out_specs=[_tile_spec(),
                   pl.BlockSpec((1, CONV_A - 1, D),
                                lambda i: (_prompt_tile_index(i) // TILES_PER_SEQ, 0, 0)),
                   pl.BlockSpec((TM, D), lambda i: (0, 0))],
        out_shape=[jax.ShapeDtypeStruct((ROWS, D), _F32),
                   jax.ShapeDtypeStruct((BATCH, CONV_A - 1, D), _F32),
                   jax.ShapeDtypeStruct((TM, D), _F32)],
        scratch_shapes=[pltpu.VMEM((_A_HIST + TM, D), _F32),
                        pltpu.VMEM((CONV_A, SUBLANES, D), _F32),
                        pltpu.VMEM((TM, D), _F32),
                        pltpu.VMEM((SUBLANES - 1, _A_SHIFT_ROWS, _A_LB), _F32)],
        compiler_params=_params(),
        name="mix_a",
    )(h, g2, w_in, cw, cb, lng, lnb, w_out, ha)


_B_HIST = 8


def _mix_b_body(h_ref, g_ref, w_in_ref, cw_ref, w_out_ref, st_ref,
                o_ref, sbp_ref, z_s_ref, ext_ref, c_ref):
    i = pl.program_id(0)
    x = h_ref[...]
    xm = _rms(x, g_ref[0:1, :]).astype(_BF)
    p = _dot(xm, w_in_ref[...])
    b_gate = p[:, :D]
    z = p[:, D:2 * D] * p[:, 2 * D:]
    w0 = cw_ref[0:1, :]
    w1 = cw_ref[1:2, :]
    w2 = cw_ref[2:3, :]

    @pl.when(i < N_PROMPT_TILES)
    def _():
        @pl.when(i % TILES_PER_SEQ == 0)
        def _():
            ext_ref[0:_B_HIST, :] = jnp.zeros((_B_HIST, D), _F32)

        ext_ref[_B_HIST:_B_HIST + TM, :] = z
        c_ref[...] = (ext_ref[_B_HIST - 2:_B_HIST - 2 + TM, :] * w0
                      + ext_ref[_B_HIST - 1:_B_HIST - 1 + TM, :] * w1
                      + z * w2)
        sbp_ref[0] = z[TM - (CONV_B - 1):, :]
        ext_ref[0:_B_HIST, :] = ext_ref[TM:TM + _B_HIST, :]

    @pl.when(i == N_PROMPT_TILES)
    def _():
        zs = [st_ref[0], st_ref[1]] + [z[l * DEC_BATCH:(l + 1) * DEC_BATCH, :] for l in range(DEC_SEQ)]
        for l in range(DEC_SEQ):
            c_ref[l * DEC_BATCH:(l + 1) * DEC_BATCH, :] = zs[l] * w0 + zs[l + 1] * w1 + zs[l + 2] * w2
        z_s_ref[...] = z[(DEC_SEQ - (CONV_B - 1)) * DEC_BATCH:, :]

    m = _dot((b_gate * c_ref[...]).astype(_BF), w_out_ref[...])
    o_ref[...] = x + _rms(m, g_ref[1:2, :])


def _mix_b(h, g2, w_in, cw, w_out, st_t):
    n_keep = CONV_B - 1
    return pl.pallas_call(
        _mix_b_body,
        grid=(N_TILES,),
        in_specs=[_tile_spec(), _const_spec((2, D)), _const_spec((D, 3 * D)),
                  _const_spec((CONV_B, D)), _const_spec((D, D)),
                  _const_spec((n_keep, DEC_BATCH, D))],
        out_specs=[_tile_spec(),
                   pl.BlockSpec((1, n_keep, D),
                                lambda i: (_prompt_tile_index(i) // TILES_PER_SEQ, 0, 0)),
                   pl.BlockSpec((n_keep * DEC_BATCH, D), lambda i: (0, 0))],
        out_shape=[jax.ShapeDtypeStruct((ROWS, D), _F32),
                   jax.ShapeDtypeStruct((BATCH, n_keep, D), _F32),
                   jax.ShapeDtypeStruct((n_keep * DEC_BATCH, D), _F32)],
        scratch_shapes=[pltpu.VMEM((_B_HIST + TM, D), _F32),
                        pltpu.VMEM((TM, D), _F32)],
        compiler_params=_params(),
        name="mix_b",
    )(h, g2, w_in, cw, w_out, st_t)


_HD = D // HEADS


def _mix_c_body(h_ref, g_ref, w_in_ref, lng_ref, lnb_ref, ws_ref, bst_ref, wsv_ref, bsv_ref, w_out_ref,
                o_ref, v_s_ref, s_ref):
    i = pl.program_id(0)
    x = h_ref[...]
    xm = _rms(x, g_ref[0:1, :]).astype(_BF)
    zz = _gelu(_dot(xm, w_in_ref[...]))
    u = zz[:, :D]
    v = _ln(zz[:, D:], lng_ref[...], lnb_ref[...])

    @pl.when(i < N_PROMPT_TILES)
    def _():
        vb = v.astype(_BF)
        n_chunks = TM // CHUNK
        row = lax.broadcasted_iota(jnp.int32, (CHUNK, CHUNK), 0)
        col = lax.broadcasted_iota(jnp.int32, (CHUNK, CHUNK), 1)
        causal = col <= row
        for hd in range(HEADS):
            lanes = slice(hd * _HD, (hd + 1) * _HD)
            wm = jnp.where(causal, ws_ref[hd], 0.0).astype(_BF)
            rhs = jnp.concatenate([vb[n * CHUNK:(n + 1) * CHUNK, lanes] for n in range(n_chunks)], axis=1)
            out = _dot(wm, rhs) + bst_ref[:, hd:hd + 1]
            for n in range(n_chunks):
                s_ref[n * CHUNK:(n + 1) * CHUNK, lanes] = out[:, n * CHUNK:(n + 1) * CHUNK]

    @pl.when(i == N_PROMPT_TILES)
    def _():
        v_s_ref[...] = v
        for l in range(DEC_SEQ):
            acc = jnp.broadcast_to(bsv_ref[l:l + 1, :], (DEC_BATCH, D))
            for lp in range(l + 1):
                acc = acc + v[lp * DEC_BATCH:(lp + 1) * DEC_BATCH, :] * wsv_ref[l, lp:lp + 1, :]
            s_ref[l * DEC_BATCH:(l + 1) * DEC_BATCH, :] = acc

    m = _dot((u * s_ref[...]).astype(_BF), w_out_ref[...])
    o_ref[...] = x + _rms(m, g_ref[1:2, :])


def _mix_c(h, g2, w_in, lng, lnb, ws, bst, wsv, bsv, w_out):
    return pl.pallas_call(
        _mix_c_body,
        grid=(N_TILES,),
        in_specs=[_tile_spec(), _const_spec((2, D)), _const_spec((D, 2 * D)),
                  _const_spec((1, D)), _const_spec((1, D)),
                  _const_spec((HEADS, CHUNK, CHUNK)), _const_spec((CHUNK, HEADS)),
                  _const_spec((DEC_SEQ, DEC_SEQ, D)), _const_spec((DEC_SEQ, D)),
                  _const_spec((D, D))],
        out_specs=[_tile_spec(), pl.BlockSpec((TM, D), lambda i: (0, 0))],
        out_shape=[jax.ShapeDtypeStruct((ROWS, D), _F32),
                   jax.ShapeDtypeStruct((TM, D), _F32)],
        scratch_shapes=[pltpu.VMEM((TM, D), _F32)],
        compiler_params=_params(),
        name="mix_c",
    )(h, g2, w_in, lng, lnb, ws, bst, wsv, bsv, w_out)


_D_HIST = 16


def _mix_d_body(h_ref, g_ref, wg_ref, scale_ref, hp_ref,
                o_ref, spp_ref, xm_s_ref, ext_ref, y_ref):
    i = pl.program_id(0)
    x = h_ref[...]
    xm = _rms(x, g_ref[0:1, :])

    @pl.when(i < N_PROMPT_TILES)
    def _():
        @pl.when(i % TILES_PER_SEQ == 0)
        def _():
            ext_ref[0:_D_HIST, :] = jnp.zeros((_D_HIST, D), _F32)

        ext_ref[_D_HIST:_D_HIST + TM, :] = xm
        pos = (i % TILES_PER_SEQ) * TM + lax.broadcasted_iota(jnp.int32, (TM, 1), 0)
        for gi, w in enumerate(POOL_WINDOWS):
            lanes = slice(gi * POOL_GD, (gi + 1) * POOL_GD)
            tot = xm[:, lanes]
            for j in range(1, w):
                tot = tot + ext_ref[_D_HIST - j:_D_HIST - j + TM, lanes]
            cnt = jnp.minimum(w, pos + 1).astype(_F32)
            diff = tot / cnt - xm[:, lanes]
            y_ref[:, lanes] = _dot(diff.astype(_BF), wg_ref[gi * POOL_GD:(gi + 1) * POOL_GD, :])
        spp_ref[0] = ext_ref[_D_HIST + TM - POOL_PAD:_D_HIST + TM, :]
        ext_ref[0:_D_HIST, :] = ext_ref[TM:TM + _D_HIST, :]

    @pl.when(i == N_PROMPT_TILES)
    def _():
        xm_s_ref[...] = xm
        for gi, w in enumerate(POOL_WINDOWS):
            lanes = slice(gi * POOL_GD, (gi + 1) * POOL_GD)
            for l in range(DEC_SEQ):
                rows = slice(l * DEC_BATCH, (l + 1) * DEC_BATCH)
                cur = xm[rows, lanes]
                tot = cur + hp_ref[rows, lanes]
                for j in range(1, min(w - 1, l) + 1):
                    tot = tot + xm[(l - j) * DEC_BATCH:(l - j + 1) * DEC_BATCH, lanes]
                diff = tot / float(w) - cur
                y_ref[rows, lanes] = _dot(diff.astype(_BF), wg_ref[gi * POOL_GD:(gi + 1) * POOL_GD, :])

    m = y_ref[...] * scale_ref[...]
    o_ref[...] = x + _rms(m, g_ref[1:2, :])


def _mix_d(h, g2, wg, scale, hp):
    return pl.pallas_call(
        _mix_d_body,
        grid=(N_TILES,),
        in_specs=[_tile_spec(), _const_spec((2, D)),
                  _const_spec((len(POOL_WINDOWS) * POOL_GD, POOL_GD)), _const_spec((1, D)),
                  _const_spec((TM, D))],
        out_specs=[_tile_spec(),
                   pl.BlockSpec((1, POOL_PAD, D),
                                lambda i: (_prompt_tile_index(i) // TILES_PER_SEQ, 0, 0)),
                   pl.BlockSpec((TM, D), lambda i: (0, 0))],
        out_shape=[jax.ShapeDtypeStruct((ROWS, D), _F32),
                   jax.ShapeDtypeStruct((BATCH, POOL_PAD, D), _F32),
                   jax.ShapeDtypeStruct((TM, D), _F32)],
        scratch_shapes=[pltpu.VMEM((_D_HIST + TM, D), _F32),
                        pltpu.VMEM((TM, D), _F32)],
        compiler_params=_params(),
        name="mix_d",
    )(h, g2, wg, scale, hp)


def _to_pos_major(a):
    return jnp.transpose(a, (1, 0, 2))


def _from_pos_major(a, length):
    return jnp.transpose(a.reshape(length, DEC_BATCH, D), (1, 0, 2))


def kernel(x_prompt, x_sample, state_conv_a, state_conv_b, state_pool, norm_g, ffn_w_in, ffn_w_down,
           a_w_in, a_conv_w, a_conv_b, a_ln_g, a_ln_b, a_w_out, b_w_in, b_conv_w, b_w_out,
           c_w_in, c_ln_g, c_ln_b, c_ws, c_bs, c_w_out, d_w_group, d_scale):
    row = lambda a: a.reshape(1, D)
    ffn_casts = lambda layer, half: _ffn_weight_casts(ffn_w_in, ffn_w_down, layer, half)

    xp = x_prompt.reshape(BATCH * SEQ, D)
    xs = _to_pos_major(x_sample).reshape(TM, D)
    ha, hp = _hist(_to_pos_major(state_conv_a), a_conv_w, _to_pos_major(state_pool))

    (h,), (w_in, w_down, a_in, a_out) = _ffn(
        (xp, xs), norm_g[0, 0:2], ffn_w_in[0, 0].astype(_BF), ffn_w_down[0, 0].astype(_BF),
        ffn_casts(0, 1) + [_matrix_cast(a_w_in), _matrix_cast(a_w_out)], split_in=True, name="ffn_0_0")
    h, sa_p, glu_s = _mix_a(h, norm_g[0, 2:4], a_in, a_conv_w, row(a_conv_b), row(a_ln_g), row(a_ln_b),
                            a_out, ha)
    (h,), (w_in, w_down) = _ffn((h,), norm_g[0, 4:6], w_in, w_down, ffn_casts(1, 0), name="ffn_0_1")

    (h,), (w_in, w_down, b_in, b_out) = _ffn(
        (h,), norm_g[1, 0:2], w_in, w_down,
        ffn_casts(1, 1) + [_matrix_cast(b_w_in), _matrix_cast(b_w_out)], name="ffn_1_0")
    h, sb_p, z_s = _mix_b(h, norm_g[1, 2:4], b_in, b_conv_w, b_out, _to_pos_major(state_conv_b))
    (h,), (w_in, w_down) = _ffn((h,), norm_g[1, 4:6], w_in, w_down, ffn_casts(2, 0), name="ffn_1_1")

    (h,), (w_in, w_down, c_in, c_out) = _ffn(
        (h,), norm_g[2, 0:2], w_in, w_down,
        ffn_casts(2, 1) + [_matrix_cast(c_w_in), _matrix_cast(c_w_out)], name="ffn_2_0")
    wsv = jnp.repeat(jnp.transpose(c_ws[:, :DEC_SEQ, :DEC_SEQ], (1, 2, 0)), _HD, axis=-1)
    bsv = jnp.repeat(jnp.transpose(c_bs[:, :DEC_SEQ]), _HD, axis=-1)
    h, v_s = _mix_c(h, norm_g[2, 2:4], c_in, row(c_ln_g), row(c_ln_b),
                    c_ws, jnp.transpose(c_bs), wsv, bsv, c_out)
    (h,), (w_in, w_down) = _ffn((h,), norm_g[2, 4:6], w_in, w_down, ffn_casts(3, 0), name="ffn_2_1")

    (h,), (w_in, w_down, d_wg) = _ffn(
        (h,), norm_g[3, 0:2], w_in, w_down,
        ffn_casts(3, 1) + [_matrix_cast(d_w_group.reshape(len(POOL_WINDOWS) * POOL_GD, POOL_GD))],
        name="ffn_3_0")
    h, sp_p, xm_s = _mix_d(h, norm_g[3, 2:4], d_wg, row(d_scale), hp)
    (yp, ys), _ = _ffn((h,), norm_g[3, 4:6], w_in, w_down, [], split_out=True, name="ffn_3_1")

    y_prompt = yp.reshape(BATCH, SEQ, D)
    y_sample = _from_pos_major(ys, DEC_SEQ)
    sa_s = jnp.concatenate([state_conv_a[:, DEC_SEQ:], _from_pos_major(glu_s, DEC_SEQ)], axis=1)
    sb_s = _from_pos_major(z_s, CONV_B - 1)
    sc_s = _from_pos_major(v_s, DEC_SEQ)
    sp_s = jnp.concatenate([state_pool[:, DEC_SEQ:], _from_pos_major(xm_s, DEC_SEQ)], axis=1)
    return (y_prompt, y_sample, sa_p, sa_s, sb_p, sb_s, sc_s, sp_p, sp_s)
```

```python
import functools
from typing import Any, NamedTuple

import jax
import jax.numpy as jnp
from jax import lax
from jax.experimental import pallas as pl
from jax.experimental.pallas import tpu as pltpu

D = 1024
D_FF = 2816
DEPTH = 4
EPS = 1e-6
FFN_HALF = 0.5
CONV_A = 31
CONV_B = 3
CHUNK = 128
HEADS = 8
POOL_WINDOWS = (2, 4, 8, 16)
POOL_GD = D // len(POOL_WINDOWS)
POOL_PAD = 15

BATCH = 8
SEQ = 2048
DEC_BATCH = 128
DEC_SEQ = 4

TM = 512
TILES_PER_SEQ = SEQ // TM
N_PROMPT_TILES = BATCH * SEQ // TM
N_TILES = N_PROMPT_TILES + 1
ROWS = N_TILES * TM
FFN_TM = 3 * TM
FF_CHUNK = 256
SUBLANES = 8
LANES = 128

FFN_CAST_STEPS = 22
MIX_CAST_STEPS = 8

_BF = jnp.bfloat16
_F32 = jnp.float32
_VMEM_LIMIT = 60 * 1024 * 1024


def _dot(a, b):
    return jnp.dot(a, b, preferred_element_type=_F32)


def _rms(x, g):
    return x * lax.rsqrt(jnp.mean(x * x, axis=-1, keepdims=True) + EPS) * g


def _ln(x, g, b):
    xc = x - jnp.mean(x, axis=-1, keepdims=True)
    return xc * lax.rsqrt(jnp.mean(xc * xc, axis=-1, keepdims=True) + EPS) * g + b


def _gelu(x):
    return 0.5 * x * (1.0 + lax.erf(x * (2.0 ** -0.5)))


def _const_spec(shape):
    zeros = (0,) * len(shape)
    return pl.BlockSpec(shape, lambda i: zeros, pipeline_mode=pl.Buffered(1))


def _tile_spec():
    return pl.BlockSpec((TM, D), lambda i: (i, 0))


def _prompt_tile_index(i):
    return jnp.minimum(i, N_PROMPT_TILES - 1)


def _params():
    return pltpu.CompilerParams(dimension_semantics=("arbitrary",),
                                vmem_limit_bytes=_VMEM_LIMIT)


class _Cast(NamedTuple):
    src: Any
    in_spec: Any
    out_spec: Any
    out_shape: Any


def _ffn_weight_casts(ffn_w_in, ffn_w_down, layer, half):
    n = FFN_CAST_STEPS
    cb = 2 * D_FF // n
    rb = D_FF // n
    step = lambda i: jnp.minimum(i, n - 1)
    return [
        _Cast(ffn_w_in,
              pl.BlockSpec((None, None, D, cb), lambda i: (layer, half, 0, step(i))),
              pl.BlockSpec((D, cb), lambda i: (0, step(i))),
              jax.ShapeDtypeStruct((D, 2 * D_FF), _BF)),
        _Cast(ffn_w_down,
              pl.BlockSpec((None, None, rb, D), lambda i: (layer, half, step(i), 0)),
              pl.BlockSpec((rb, D), lambda i: (step(i), 0)),
              jax.ShapeDtypeStruct((D_FF, D), _BF)),
    ]


def _matrix_cast(w):
    n = MIX_CAST_STEPS
    rows, cols = w.shape
    rb = rows // n
    step = lambda i: jnp.minimum(i, n - 1)
    return _Cast(w,
                 pl.BlockSpec((rb, cols), lambda i: (step(i), 0)),
                 pl.BlockSpec((rb, cols), lambda i: (step(i), 0)),
                 jax.ShapeDtypeStruct((rows, cols), _BF))


def _run_casts(in_refs, out_refs):
    for src_ref, dst_ref in zip(in_refs, out_refs):
        dst_ref[...] = src_ref[...].astype(_BF)


def _unpack(refs, n_in, n_out, n_cast):
    refs = list(refs)
    o0 = n_in + n_cast
    s0 = o0 + n_out + n_cast
    return refs[:n_in], refs[n_in:o0], refs[o0:o0 + n_out], refs[o0 + n_out:s0], refs[s0:]


def _call(body, *, grid, in_specs, args, out_specs, out_shapes, casts, scratch_shapes=(), name):
    outs = pl.pallas_call(
        functools.partial(body, n_cast=len(casts)),
        grid=(grid,),
        in_specs=list(in_specs) + [c.in_spec for c in casts],
        out_specs=list(out_specs) + [c.out_spec for c in casts],
        out_shape=list(out_shapes) + [c.out_shape for c in casts],
        scratch_shapes=list(scratch_shapes),
        compiler_params=_params(),
        name=name,
    )(*args, *[c.src for c in casts])
    n_out = len(out_specs)
    return outs[:n_out], outs[n_out:]


def _ffn_math(x, g_ref, w_in_ref, w_down_ref):
    xn = _rms(x, g_ref[0:1, :]).astype(_BF)
    acc = None
    for c in range(D_FF // FF_CHUNK):
        c0 = c * FF_CHUNK
        gate = _dot(xn, w_in_ref[:, c0:c0 + FF_CHUNK])
        up = _dot(xn, w_in_ref[:, D_FF + c0:D_FF + c0 + FF_CHUNK])
        act = (gate * jax.nn.sigmoid(gate) * up).astype(_BF)
        d = _dot(act, w_down_ref[c0:c0 + FF_CHUNK, :])
        acc = d if acc is None else acc + d
    return x + FFN_HALF * _rms(acc, g_ref[1:2, :])


def _ffn_body(*refs, split_in, split_out, tm, n_cast):
    n_x = 2 if split_in else 1
    ins, cast_in, outs, cast_out, _ = _unpack(refs, n_x + 3, 2 if split_out else 1, n_cast)
    x_refs = ins[:n_x]
    g_ref, w_in_ref, w_down_ref = ins[n_x:]
    i = pl.program_id(0)
    for sub in range(tm // TM):
        rows = slice(sub * TM, (sub + 1) * TM)
        if split_in:
            x = jnp.where(i < N_PROMPT_TILES, x_refs[0][rows, :], x_refs[1][rows, :])
        else:
            x = x_refs[0][rows, :]
        y = _ffn_math(x, g_ref, w_in_ref, w_down_ref)
        if split_out:
            @pl.when(i < N_PROMPT_TILES)
            def _():
                outs[0][rows, :] = y

            @pl.when(i == N_PROMPT_TILES)
            def _():
                outs[1][rows, :] = y
        else:
            outs[0][rows, :] = y
    _run_casts(cast_in, cast_out)


def _ffn(xs, g2, w_in, w_down, casts=(), *, split_in=False, split_out=False, tm=TM, name):
    split_specs = [pl.BlockSpec((TM, D), lambda i: (_prompt_tile_index(i), 0)),
                   pl.BlockSpec((TM, D), lambda i: (0, 0))]
    split_shapes = [jax.ShapeDtypeStruct((N_PROMPT_TILES * TM, D), _F32),
                    jax.ShapeDtypeStruct((TM, D), _F32)]
    tile = pl.BlockSpec((tm, D), lambda i: (i, 0))
    assert tm == TM or not (split_in or split_out or casts)
    return _call(
        functools.partial(_ffn_body, split_in=split_in, split_out=split_out, tm=tm),
        grid=ROWS // tm,
        in_specs=(split_specs if split_in else [tile])
        + [_const_spec((2, D)), _const_spec((D, 2 * D_FF)), _const_spec((D_FF, D))],
        args=(*xs, g2, w_in, w_down),
        out_specs=split_specs if split_out else [tile],
        out_shapes=split_shapes if split_out else [jax.ShapeDtypeStruct((ROWS, D), _F32)],
        casts=list(casts), name=name)


_HIST_BB = 32


def _hist_body(sta_ref, cw_ref, stp_ref, ha_ref, hp_ref):
    for l in range(DEC_SEQ):
        acc = jnp.zeros((_HIST_BB, D), _F32)
        for j in range(l, CONV_A - 1):
            acc = acc + sta_ref[j] * cw_ref[j - l:j - l + 1, :]
        ha_ref[l] = acc
        for gi, w in enumerate(POOL_WINDOWS):
            lanes = slice(gi * POOL_GD, (gi + 1) * POOL_GD)
            tot = jnp.zeros((_HIST_BB, POOL_GD), _F32)
            for j in range(l + 1, w):
                tot = tot + stp_ref[POOL_PAD + l - j, :, lanes]
            hp_ref[l, :, lanes] = tot


def _hist(st_a_t, cw, st_p_t):
    out = jax.ShapeDtypeStruct((DEC_SEQ, DEC_BATCH, D), _F32)
    ha, hp = pl.pallas_call(
        _hist_body,
        grid=(DEC_BATCH // _HIST_BB,),
        in_specs=[pl.BlockSpec((CONV_A - 1, _HIST_BB, D), lambda i: (0, i, 0)),
                  pl.BlockSpec((CONV_A, D), lambda i: (0, 0)),
                  pl.BlockSpec((POOL_PAD, _HIST_BB, D), lambda i: (0, i, 0))],
        out_specs=[pl.BlockSpec((DEC_SEQ, _HIST_BB, D), lambda i: (0, i, 0)),
                   pl.BlockSpec((DEC_SEQ, _HIST_BB, D), lambda i: (0, i, 0))],
        out_shape=[out, out],
        compiler_params=_params(),
        name="sample_hist",
    )(st_a_t, cw, st_p_t)
    return ha.reshape(TM, D), hp.reshape(TM, D)


_A_HIST = 32
_A_FIRST = _A_HIST - (CONV_A - 1)
_A_LB = 256
_A_RB = 64
_A_SHIFT_ROWS = 536


def _mix_a_body(*refs, n_cast):
    ins, cast_in, outs, cast_out, scratch = _unpack(refs, 9, 3, n_cast)
    h_ref, g_ref, w_in_ref, cw_ref, cb_ref, lng_ref, lnb_ref, w_out_ref, ha_ref = ins
    o_ref, sap_ref, glu_s_ref = outs
    ext_ref, cwb_ref, c_ref, sh_ref = scratch
    i = pl.program_id(0)

    @pl.when(i == 0)
    def _():
        for k in range(CONV_A):
            cwb_ref[k] = jnp.broadcast_to(cw_ref[k:k + 1, :], (SUBLANES, D))

    x = h_ref[...]
    xm = _rms(x, g_ref[0:1, :]).astype(_BF)
    a = _dot(xm, w_in_ref[...])
    glu = a[:, :D] * jax.nn.sigmoid(a[:, D:])

    def blocks(v, rows, lanes):
        return v.reshape(rows // SUBLANES, SUBLANES, lanes)

    @pl.when(i < N_PROMPT_TILES)
    def _():
        @pl.when(i % TILES_PER_SEQ == 0)
        def _():
            ext_ref[0:_A_HIST, :] = jnp.zeros((_A_HIST, D), _F32)

        ext_ref[_A_HIST:_A_HIST + TM, :] = glu
        for lb in range(D // _A_LB):
            lanes = slice(lb * _A_LB, (lb + 1) * _A_LB)
            for r in range(1, SUBLANES):
                sh_ref[r - 1] = ext_ref[r:r + _A_SHIFT_ROWS, lanes]
            bias = jnp.broadcast_to(cb_ref[:, lanes], (SUBLANES, _A_LB))[None]
            for rb in range(TM // _A_RB):
                r0 = rb * _A_RB
                acc = jnp.broadcast_to(bias, (_A_RB // SUBLANES, SUBLANES, _A_LB))
                for k in range(CONV_A):
                    r = (_A_FIRST + k) % SUBLANES
                    s = r0 + _A_FIRST + k - r
                    if r == 0:
                        rows = ext_ref[s:s + _A_RB, lanes]
                    else:
                        rows = sh_ref[r - 1, s:s + _A_RB, :]
                    acc = acc + blocks(rows, _A_RB, _A_LB) * cwb_ref[k][:, lanes][None]
                c_ref[r0:r0 + _A_RB, lanes] = acc.reshape(_A_RB, _A_LB)
        sap_ref[0] = ext_ref[TM + _A_FIRST:TM + _A_HIST, :]
        ext_ref[0:_A_HIST, :] = ext_ref[TM:TM + _A_HIST, :]

    @pl.when(i == N_PROMPT_TILES)
    def _():
        glu_s_ref[...] = glu
        ext_ref[0:TM, :] = glu
        bias = jnp.broadcast_to(cb_ref[...], (SUBLANES, D))[None]
        for l in range(DEC_SEQ):
            for rb in range(DEC_BATCH // _A_RB):
                r0 = l * DEC_BATCH + rb * _A_RB
                acc = blocks(ha_ref[r0:r0 + _A_RB, :], _A_RB, D) + bias
                for lp in range(l + 1):
                    s = lp * DEC_BATCH + rb * _A_RB
                    acc = acc + blocks(ext_ref[s:s + _A_RB, :], _A_RB, D) * cwb_ref[CONV_A - 1 + lp - l][None]
                c_ref[r0:r0 + _A_RB, :] = acc.reshape(_A_RB, D)

    c = _ln(c_ref[...], lng_ref[...], lnb_ref[...])
    c = c * jax.nn.sigmoid(c)
    m = _dot(c.astype(_BF), w_out_ref[...])
    o_ref[...] = x + _rms(m, g_ref[1:2, :])
    _run_casts(cast_in, cast_out)


def _mix_a(h, g2, w_in, cw, cb, lng, lnb, w_out, ha, casts):
    return _call(
        _mix_a_body,
        grid=N_TILES,
        in_specs=[_tile_spec(), _const_spec((2, D)), _const_spec((D, 2 * D)),
                  _const_spec((CONV_A, D)), _const_spec((1, D)), _const_spec((1, D)),
                  _const_spec((1, D)), _const_spec((D, D)), _const_spec((TM, D))],
        args=(h, g2, w_in, cw, cb, lng, lnb, w_out, ha),
        out_specs=[_tile_spec(),
                   pl.BlockSpec((1, CONV_A - 1, D),
                                lambda i: (_prompt_tile_index(i) // TILES_PER_SEQ, 0, 0)),
                   pl.BlockSpec((TM, D), lambda i: (0, 0))],
        out_shapes=[jax.ShapeDtypeStruct((ROWS, D), _F32),
                    jax.ShapeDtypeStruct((BATCH, CONV_A - 1, D), _F32),
                    jax.ShapeDtypeStruct((TM, D), _F32)],
        scratch_shapes=[pltpu.VMEM((_A_HIST + TM, D), _F32),
                        pltpu.VMEM((CONV_A, SUBLANES, D), _F32),
                        pltpu.VMEM((TM, D), _F32),
                        pltpu.VMEM((SUBLANES - 1, _A_SHIFT_ROWS, _A_LB), _F32)],
        casts=casts, name="mix_a")


_B_HIST = 8


def _mix_b_body(*refs, n_cast):
    ins, cast_in, outs, cast_out, scratch = _unpack(refs, 6, 3, n_cast)
    h_ref, g_ref, w_in_ref, cw_ref, w_out_ref, st_ref = ins
    o_ref, sbp_ref, z_s_ref = outs
    ext_ref, c_ref = scratch
    i = pl.program_id(0)
    x = h_ref[...]
    xm = _rms(x, g_ref[0:1, :]).astype(_BF)
    p = _dot(xm, w_in_ref[...])
    b_gate = p[:, :D]
    z = p[:, D:2 * D] * p[:, 2 * D:]
    w0 = cw_ref[0:1, :]
    w1 = cw_ref[1:2, :]
    w2 = cw_ref[2:3, :]

    @pl.when(i < N_PROMPT_TILES)
    def _():
        @pl.when(i % TILES_PER_SEQ == 0)
        def _():
            ext_ref[0:_B_HIST, :] = jnp.zeros((_B_HIST, D), _F32)

        ext_ref[_B_HIST:_B_HIST + TM, :] = z
        c_ref[...] = (ext_ref[_B_HIST - 2:_B_HIST - 2 + TM, :] * w0
                      + ext_ref[_B_HIST - 1:_B_HIST - 1 + TM, :] * w1
                      + z * w2)
        sbp_ref[0] = z[TM - (CONV_B - 1):, :]
        ext_ref[0:_B_HIST, :] = ext_ref[TM:TM + _B_HIST, :]

    @pl.when(i == N_PROMPT_TILES)
    def _():
        zs = [st_ref[0], st_ref[1]] + [z[l * DEC_BATCH:(l + 1) * DEC_BATCH, :] for l in range(DEC_SEQ)]
        for l in range(DEC_SEQ):
            c_ref[l * DEC_BATCH:(l + 1) * DEC_BATCH, :] = zs[l] * w0 + zs[l + 1] * w1 + zs[l + 2] * w2
        z_s_ref[...] = z[(DEC_SEQ - (CONV_B - 1)) * DEC_BATCH:, :]

    m = _dot((b_gate * c_ref[...]).astype(_BF), w_out_ref[...])
    o_ref[...] = x + _rms(m, g_ref[1:2, :])
    _run_casts(cast_in, cast_out)


def _mix_b(h, g2, w_in, cw, w_out, st_t, casts):
    n_keep = CONV_B - 1
    return _call(
        _mix_b_body,
        grid=N_TILES,
        in_specs=[_tile_spec(), _const_spec((2, D)), _const_spec((D, 3 * D)),
                  _const_spec((CONV_B, D)), _const_spec((D, D)),
                  _const_spec((n_keep, DEC_BATCH, D))],
        args=(h, g2, w_in, cw, w_out, st_t),
        out_specs=[_tile_spec(),
                   pl.BlockSpec((1, n_keep, D),
                                lambda i: (_prompt_tile_index(i) // TILES_PER_SEQ, 0, 0)),
                   pl.BlockSpec((n_keep * DEC_BATCH, D), lambda i: (0, 0))],
        out_shapes=[jax.ShapeDtypeStruct((ROWS, D), _F32),
                    jax.ShapeDtypeStruct((BATCH, n_keep, D), _F32),
                    jax.ShapeDtypeStruct((n_keep * DEC_BATCH, D), _F32)],
        scratch_shapes=[pltpu.VMEM((_B_HIST + TM, D), _F32),
                        pltpu.VMEM((TM, D), _F32)],
        casts=casts, name="mix_b")


_HD = D // HEADS


def _mix_c_body(*refs, n_cast):
    ins, cast_in, outs, cast_out, scratch = _unpack(refs, 10, 2, n_cast)
    h_ref, g_ref, w_in_ref, lng_ref, lnb_ref, ws_ref, bst_ref, wsv_ref, bsv_ref, w_out_ref = ins
    o_ref, v_s_ref = outs
    (s_ref,) = scratch
    i = pl.program_id(0)
    x = h_ref[...]
    xm = _rms(x, g_ref[0:1, :]).astype(_BF)
    zz = _gelu(_dot(xm, w_in_ref[...]))
    u = zz[:, :D]
    v = _ln(zz[:, D:], lng_ref[...], lnb_ref[...])

    @pl.when(i < N_PROMPT_TILES)
    def _():
        vb = v.astype(_BF)
        n_chunks = TM // CHUNK
        row = lax.broadcasted_iota(jnp.int32, (CHUNK, CHUNK), 0)
        col = lax.broadcasted_iota(jnp.int32, (CHUNK, CHUNK), 1)
        causal = col <= row
        for hd in range(HEADS):
            lanes = slice(hd * _HD, (hd + 1) * _HD)
            wm = jnp.where(causal, ws_ref[hd], 0.0).astype(_BF)
            rhs = jnp.concatenate([vb[n * CHUNK:(n + 1) * CHUNK, lanes] for n in range(n_chunks)], axis=1)
            out = _dot(wm, rhs) + bst_ref[:, hd:hd + 1]
            for n in range(n_chunks):
                s_ref[n * CHUNK:(n + 1) * CHUNK, lanes] = out[:, n * CHUNK:(n + 1) * CHUNK]

    @pl.when(i == N_PROMPT_TILES)
    def _():
        v_s_ref[...] = v
        for l in range(DEC_SEQ):
            acc = jnp.broadcast_to(bsv_ref[l:l + 1, :], (DEC_BATCH, D))
            for lp in range(l + 1):
                acc = acc + v[lp * DEC_BATCH:(lp + 1) * DEC_BATCH, :] * wsv_ref[l, lp:lp + 1, :]
            s_ref[l * DEC_BATCH:(l + 1) * DEC_BATCH, :] = acc

    m = _dot((u * s_ref[...]).astype(_BF), w_out_ref[...])
    o_ref[...] = x + _rms(m, g_ref[1:2, :])
    _run_casts(cast_in, cast_out)


def _mix_c(h, g2, w_in, lng, lnb, ws, bst, wsv, bsv, w_out, casts):
    return _call(
        _mix_c_body,
        grid=N_TILES,
        in_specs=[_tile_spec(), _const_spec((2, D)), _const_spec((D, 2 * D)),
                  _const_spec((1, D)), _const_spec((1, D)),
                  _const_spec((HEADS, CHUNK, CHUNK)), _const_spec((CHUNK, HEADS)),
                  _const_spec((DEC_SEQ, DEC_SEQ, D)), _const_spec((DEC_SEQ, D)),
                  _const_spec((D, D))],
        args=(h, g2, w_in, lng, lnb, ws, bst, wsv, bsv, w_out),
        out_specs=[_tile_spec(), pl.BlockSpec((TM, D), lambda i: (0, 0))],
        out_shapes=[jax.ShapeDtypeStruct((ROWS, D), _F32),
                    jax.ShapeDtypeStruct((TM, D), _F32)],
        scratch_shapes=[pltpu.VMEM((TM, D), _F32)],
        casts=casts, name="mix_c")


_D_ZERO = 8
_D_HIST = 24
_D_ROWS = _D_HIST + TM


def _mix_d_body(*refs, n_cast):
    ins, cast_in, outs, cast_out, scratch = _unpack(refs, 5, 3, n_cast)
    h_ref, g_ref, wg_ref, scale_ref, hp_ref = ins
    o_ref, spp_ref, xm_s_ref = outs
    ext_ref, sa_ref, sb_ref, y_ref = scratch
    i = pl.program_id(0)
    x = h_ref[...]
    xm = _rms(x, g_ref[0:1, :])
    g1, g2, g3 = POOL_GD, 2 * POOL_GD, 3 * POOL_GD

    @pl.when(i == 0)
    def _():
        for ref in (ext_ref, sa_ref, sb_ref):
            ref[0:_D_ZERO, :] = jnp.zeros((_D_ZERO, D), _F32)

    @pl.when(i < N_PROMPT_TILES)
    def _():
        @pl.when(i % TILES_PER_SEQ == 0)
        def _():
            ext_ref[_D_ZERO:_D_HIST, :] = jnp.zeros((_D_HIST - _D_ZERO, D), _F32)

        ext_ref[_D_HIST:_D_ROWS, :] = xm
        lo, hi = _D_ZERO, _D_ROWS
        sa_ref[lo:hi, :] = ext_ref[lo:hi, :] + ext_ref[lo - 1:hi - 1, :]
        sb_ref[lo:hi, g1:] = sa_ref[lo:hi, g1:] + sa_ref[lo - 2:hi - 2, g1:]
        sa_ref[lo:hi, g2:] = sb_ref[lo:hi, g2:] + sb_ref[lo - 4:hi - 4, g2:]
        tots = (sa_ref[_D_HIST:hi, 0:g1], sb_ref[_D_HIST:hi, g1:g2], sa_ref[_D_HIST:hi, g2:g3],
                sa_ref[_D_HIST:hi, g3:] + sa_ref[_D_HIST - 8:hi - 8, g3:])
        pos = (i % TILES_PER_SEQ) * TM + lax.broadcasted_iota(jnp.int32, (TM, 1), 0)
        for gi, w in enumerate(POOL_WINDOWS):
            lanes = slice(gi * POOL_GD, (gi + 1) * POOL_GD)
            cnt = jnp.minimum(w, pos + 1).astype(_F32)
            diff = tots[gi] / cnt - xm[:, lanes]
            y_ref[:, lanes] = _dot(diff.astype(_BF), wg_ref[gi * POOL_GD:(gi + 1) * POOL_GD, :])
        spp_ref[0] = ext_ref[_D_ROWS - POOL_PAD:_D_ROWS, :]
        ext_ref[_D_ZERO:_D_HIST, :] = ext_ref[TM + _D_ZERO:_D_ROWS, :]

    @pl.when(i == N_PROMPT_TILES)
    def _():
        xm_s_ref[...] = xm
        for gi, w in enumerate(POOL_WINDOWS):
            lanes = slice(gi * POOL_GD, (gi + 1) * POOL_GD)
            for l in range(DEC_SEQ):
                rows = slice(l * DEC_BATCH, (l + 1) * DEC_BATCH)
                cur = xm[rows, lanes]
                tot = cur + hp_ref[rows, lanes]
                for j in range(1, min(w - 1, l) + 1):
                    tot = tot + xm[(l - j) * DEC_BATCH:(l - j + 1) * DEC_BATCH, lanes]
                diff = tot / float(w) - cur
                y_ref[rows, lanes] = _dot(diff.astype(_BF), wg_ref[gi * POOL_GD:(gi + 1) * POOL_GD, :])

    m = y_ref[...] * scale_ref[...]
    o_ref[...] = x + _rms(m, g_ref[1:2, :])
    _run_casts(cast_in, cast_out)


def _mix_d(h, g2, wg, scale, hp, casts):
    return _call(
        _mix_d_body,
        grid=N_TILES,
        in_specs=[_tile_spec(), _const_spec((2, D)),
                  _const_spec((len(POOL_WINDOWS) * POOL_GD, POOL_GD)), _const_spec((1, D)),
                  _const_spec((TM, D))],
        args=(h, g2, wg, scale, hp),
        out_specs=[_tile_spec(),
                   pl.BlockSpec((1, POOL_PAD, D),
                                lambda i: (_prompt_tile_index(i) // TILES_PER_SEQ, 0, 0)),
                   pl.BlockSpec((TM, D), lambda i: (0, 0))],
        out_shapes=[jax.ShapeDtypeStruct((ROWS, D), _F32),
                    jax.ShapeDtypeStruct((BATCH, POOL_PAD, D), _F32),
                    jax.ShapeDtypeStruct((TM, D), _F32)],
        scratch_shapes=[pltpu.VMEM((_D_ROWS, D), _F32), pltpu.VMEM((_D_ROWS, D), _F32),
                        pltpu.VMEM((_D_ROWS, D), _F32), pltpu.VMEM((TM, D), _F32)],
        casts=casts, name="mix_d")


def _to_pos_major(a):
    return jnp.transpose(a, (1, 0, 2))


def _from_pos_major(a, length):
    return jnp.transpose(a.reshape(length, DEC_BATCH, D), (1, 0, 2))


def kernel(x_prompt, x_sample, state_conv_a, state_conv_b, state_pool, norm_g, ffn_w_in, ffn_w_down,
           a_w_in, a_conv_w, a_conv_b, a_ln_g, a_ln_b, a_w_out, b_w_in, b_conv_w, b_w_out,
           c_w_in, c_ln_g, c_ln_b, c_ws, c_bs, c_w_out, d_w_group, d_scale):
    row = lambda a: a.reshape(1, D)
    ffn_casts = lambda layer, half: _ffn_weight_casts(ffn_w_in, ffn_w_down, layer, half)
    mats = lambda *ws: [_matrix_cast(w) for w in ws]

    xp = x_prompt.reshape(BATCH * SEQ, D)
    xs = _to_pos_major(x_sample).reshape(TM, D)
    ha, hp = _hist(_to_pos_major(state_conv_a), a_conv_w, _to_pos_major(state_pool))

    (h,), (a_in, a_out, w01_in, w01_down, b_in, b_out) = _ffn(
        (xp, xs), norm_g[0, 0:2], ffn_w_in[0, 0].astype(_BF), ffn_w_down[0, 0].astype(_BF),
        mats(a_w_in, a_w_out) + ffn_casts(0, 1) + mats(b_w_in, b_w_out), split_in=True, name="ffn_0_0")
    (h, sa_p, glu_s), (w10_in, w10_down) = _mix_a(
        h, norm_g[0, 2:4], a_in, a_conv_w, row(a_conv_b), row(a_ln_g), row(a_ln_b), a_out, ha,
        ffn_casts(1, 0))
    (h,), _ = _ffn((h,), norm_g[0, 4:6], w01_in, w01_down, tm=FFN_TM, name="ffn_0_1")

    (h,), _ = _ffn((h,), norm_g[1, 0:2], w10_in, w10_down, tm=FFN_TM, name="ffn_1_0")
    (h, sb_p, z_s), (w11_in, w11_down, w20_in, w20_down, c_in, c_out) = _mix_b(
        h, norm_g[1, 2:4], b_in, b_conv_w, b_out, _to_pos_major(state_conv_b),
        ffn_casts(1, 1) + ffn_casts(2, 0) + mats(c_w_in, c_w_out))
    (h,), _ = _ffn((h,), norm_g[1, 4:6], w11_in, w11_down, tm=FFN_TM, name="ffn_1_1")

    (h,), _ = _ffn((h,), norm_g[2, 0:2], w20_in, w20_down, tm=FFN_TM, name="ffn_2_0")
    wsv = jnp.repeat(jnp.transpose(c_ws[:, :DEC_SEQ, :DEC_SEQ], (1, 2, 0)), _HD, axis=-1)
    bsv = jnp.repeat(jnp.transpose(c_bs[:, :DEC_SEQ]), _HD, axis=-1)
    (h, v_s), (w21_in, w21_down, w30_in, w30_down, d_wg) = _mix_c(
        h, norm_g[2, 2:4], c_in, row(c_ln_g), row(c_ln_b), c_ws, jnp.transpose(c_bs), wsv, bsv, c_out,
        ffn_casts(2, 1) + ffn_casts(3, 0)
        + mats(d_w_group.reshape(len(POOL_WINDOWS) * POOL_GD, POOL_GD)))
    (h,), _ = _ffn((h,), norm_g[2, 4:6], w21_in, w21_down, tm=FFN_TM, name="ffn_2_1")

    (h,), _ = _ffn((h,), norm_g[3, 0:2], w30_in, w30_down, tm=FFN_TM, name="ffn_3_0")
    (h, sp_p, xm_s), (w31_in, w31_down) = _mix_d(h, norm_g[3, 2:4], d_wg, row(d_scale), hp, ffn_casts(3, 1))
    (yp, ys), _ = _ffn((h,), norm_g[3, 4:6], w31_in, w31_down, split_out=True, name="ffn_3_1")

    y_prompt = yp.reshape(BATCH, SEQ, D)
    y_sample = _from_pos_major(ys, DEC_SEQ)
    sa_s = jnp.concatenate([state_conv_a[:, DEC_SEQ:], _from_pos_major(glu_s, DEC_SEQ)], axis=1)
    sb_s = _from_pos_major(z_s, CONV_B - 1)
    sc_s = _from_pos_major(v_s, DEC_SEQ)
    sp_s = jnp.concatenate([state_pool[:, DEC_SEQ:], _from_pos_major(xm_s, DEC_SEQ)], axis=1)
    return (y_prompt, y_sample, sa_p, sa_s, sb_p, sb_s, sc_s, sp_p, sp_s)
```

```python
import functools
from typing import Any, NamedTuple

import jax
import jax.numpy as jnp
from jax import lax
from jax.experimental import pallas as pl
from jax.experimental.pallas import tpu as pltpu

D = 1024
D_FF = 2816
DEPTH = 4
EPS = 1e-6
FFN_HALF = 0.5
CONV_A = 31
CONV_B = 3
CHUNK = 128
HEADS = 8
POOL_WINDOWS = (2, 4, 8, 16)
POOL_GD = D // len(POOL_WINDOWS)
POOL_PAD = 15

BATCH = 8
SEQ = 2048
DEC_BATCH = 128
DEC_SEQ = 4

TM = 512
TILES_PER_SEQ = SEQ // TM
N_PROMPT_TILES = BATCH * SEQ // TM
N_TILES = N_PROMPT_TILES + 1
ROWS = N_TILES * TM
FF_CHUNK = 256
SUBLANES = 8
LANES = 128

FFN_CAST_STEPS = 11
MIX_CAST_STEPS = 4

_BF = jnp.bfloat16
_F32 = jnp.float32
_VMEM_LIMIT = 60 * 1024 * 1024


def _dot(a, b):
    return jnp.dot(a, b, preferred_element_type=_F32)


def _rms(x, g):
    return x * lax.rsqrt(jnp.mean(x * x, axis=-1, keepdims=True) + EPS) * g


def _ln(x, g, b):
    xc = x - jnp.mean(x, axis=-1, keepdims=True)
    return xc * lax.rsqrt(jnp.mean(xc * xc, axis=-1, keepdims=True) + EPS) * g + b


def _gelu(x):
    return 0.5 * x * (1.0 + lax.erf(x * (2.0 ** -0.5)))


def _const_spec(shape):
    zeros = (0,) * len(shape)
    return pl.BlockSpec(shape, lambda i: zeros, pipeline_mode=pl.Buffered(1))


def _tile_spec():
    return pl.BlockSpec((TM, D), lambda i: (i, 0))


def _prompt_tile_index(i):
    return jnp.minimum(i, N_PROMPT_TILES - 1)


def _params():
    return pltpu.CompilerParams(dimension_semantics=("arbitrary",),
                                vmem_limit_bytes=_VMEM_LIMIT)


class _Cast(NamedTuple):
    src: Any
    in_spec: Any
    out_spec: Any
    out_shape: Any


def _ffn_weight_casts(ffn_w_in, ffn_w_down, layer, half):
    n = FFN_CAST_STEPS
    cb = 2 * D_FF // n
    rb = D_FF // n
    step = lambda i: jnp.minimum(i, n - 1)
    return [
        _Cast(ffn_w_in,
              pl.BlockSpec((None, None, D, cb), lambda i: (layer, half, 0, step(i))),
              pl.BlockSpec((D, cb), lambda i: (0, step(i))),
              jax.ShapeDtypeStruct((D, 2 * D_FF), _BF)),
        _Cast(ffn_w_down,
              pl.BlockSpec((None, None, rb, D), lambda i: (layer, half, step(i), 0)),
              pl.BlockSpec((rb, D), lambda i: (step(i), 0)),
              jax.ShapeDtypeStruct((D_FF, D), _BF)),
    ]


def _matrix_cast(w):
    n = MIX_CAST_STEPS
    rows, cols = w.shape
    rb = rows // n
    step = lambda i: jnp.minimum(i, n - 1)
    return _Cast(w,
                 pl.BlockSpec((rb, cols), lambda i: (step(i), 0)),
                 pl.BlockSpec((rb, cols), lambda i: (step(i), 0)),
                 jax.ShapeDtypeStruct((rows, cols), _BF))


def _run_casts(in_refs, out_refs):
    for src_ref, dst_ref in zip(in_refs, out_refs):
        dst_ref[...] = src_ref[...].astype(_BF)


def _unpack(refs, n_in, n_out, n_cast):
    refs = list(refs)
    o0 = n_in + n_cast
    s0 = o0 + n_out + n_cast
    return refs[:n_in], refs[n_in:o0], refs[o0:o0 + n_out], refs[o0 + n_out:s0], refs[s0:]


def _call(body, *, grid, in_specs, args, out_specs, out_shapes, casts, scratch_shapes=(), name):
    outs = pl.pallas_call(
        functools.partial(body, n_cast=len(casts)),
        grid=(grid,),
        in_specs=list(in_specs) + [c.in_spec for c in casts],
        out_specs=list(out_specs) + [c.out_spec for c in casts],
        out_shape=list(out_shapes) + [c.out_shape for c in casts],
        scratch_shapes=list(scratch_shapes),
        compiler_params=_params(),
        name=name,
    )(*args, *[c.src for c in casts])
    n_out = len(out_specs)
    return outs[:n_out], outs[n_out:]


def _ffn_math(x, g_ref, w_in_ref, w_down_ref):
    xn = _rms(x, g_ref[0:1, :]).astype(_BF)
    acc = None
    for c in range(D_FF // FF_CHUNK):
        c0 = c * FF_CHUNK
        gate = _dot(xn, w_in_ref[:, c0:c0 + FF_CHUNK])
        up = _dot(xn, w_in_ref[:, D_FF + c0:D_FF + c0 + FF_CHUNK])
        act = (gate * jax.nn.sigmoid(gate) * up).astype(_BF)
        d = _dot(act, w_down_ref[c0:c0 + FF_CHUNK, :])
        acc = d if acc is None else acc + d
    return x + FFN_HALF * _rms(acc, g_ref[1:2, :])


def _ffn_body(*refs, split_in, split_out, n_cast):
    n_x = 2 if split_in else 1
    ins, cast_in, outs, cast_out, _ = _unpack(refs, n_x + 3, 2 if split_out else 1, n_cast)
    x_refs = ins[:n_x]
    g_ref, w_in_ref, w_down_ref = ins[n_x:]
    i = pl.program_id(0)
    if split_in:
        x = jnp.where(i < N_PROMPT_TILES, x_refs[0][...], x_refs[1][...])
    else:
        x = x_refs[0][...]
    y = _ffn_math(x, g_ref, w_in_ref, w_down_ref)
    if split_out:
        @pl.when(i < N_PROMPT_TILES)
        def _():
            outs[0][...] = y

        @pl.when(i == N_PROMPT_TILES)
        def _():
            outs[1][...] = y
    else:
        outs[0][...] = y
    _run_casts(cast_in, cast_out)


def _ffn(xs, g2, w_in, w_down, casts=(), *, split_in=False, split_out=False, name):
    split_specs = [pl.BlockSpec((TM, D), lambda i: (_prompt_tile_index(i), 0)),
                   pl.BlockSpec((TM, D), lambda i: (0, 0))]
    split_shapes = [jax.ShapeDtypeStruct((N_PROMPT_TILES * TM, D), _F32),
                    jax.ShapeDtypeStruct((TM, D), _F32)]
    return _call(
        functools.partial(_ffn_body, split_in=split_in, split_out=split_out),
        grid=N_TILES,
        in_specs=(split_specs if split_in else [_tile_spec()])
        + [_const_spec((2, D)), _const_spec((D, 2 * D_FF)), _const_spec((D_FF, D))],
        args=(*xs, g2, w_in, w_down),
        out_specs=split_specs if split_out else [_tile_spec()],
        out_shapes=split_shapes if split_out else [jax.ShapeDtypeStruct((ROWS, D), _F32)],
        casts=list(casts), name=name)


_HIST_BB = 32


def _hist_body(sta_ref, cw_ref, stp_ref, ha_ref, hp_ref):
    for l in range(DEC_SEQ):
        acc = jnp.zeros((_HIST_BB, D), _F32)
        for j in range(l, CONV_A - 1):
            acc = acc + sta_ref[j] * cw_ref[j - l:j - l + 1, :]
        ha_ref[l] = acc
        for gi, w in enumerate(POOL_WINDOWS):
            lanes = slice(gi * POOL_GD, (gi + 1) * POOL_GD)
            tot = jnp.zeros((_HIST_BB, POOL_GD), _F32)
            for j in range(l + 1, w):
                tot = tot + stp_ref[POOL_PAD + l - j, :, lanes]
            hp_ref[l, :, lanes] = tot


def _hist(st_a_t, cw, st_p_t):
    out = jax.ShapeDtypeStruct((DEC_SEQ, DEC_BATCH, D), _F32)
    ha, hp = pl.pallas_call(
        _hist_body,
        grid=(DEC_BATCH // _HIST_BB,),
        in_specs=[pl.BlockSpec((CONV_A - 1, _HIST_BB, D), lambda i: (0, i, 0)),
                  pl.BlockSpec((CONV_A, D), lambda i: (0, 0)),
                  pl.BlockSpec((POOL_PAD, _HIST_BB, D), lambda i: (0, i, 0))],
        out_specs=[pl.BlockSpec((DEC_SEQ, _HIST_BB, D), lambda i: (0, i, 0)),
                   pl.BlockSpec((DEC_SEQ, _HIST_BB, D), lambda i: (0, i, 0))],
        out_shape=[out, out],
        compiler_params=_params(),
        name="sample_hist",
    )(st_a_t, cw, st_p_t)
    return ha.reshape(TM, D), hp.reshape(TM, D)


_A_HIST = 32
_A_FIRST = _A_HIST - (CONV_A - 1)
_A_LB = 256
_A_RB = 64
_A_SHIFT_ROWS = 536


def _mix_a_body(*refs, n_cast):
    ins, cast_in, outs, cast_out, scratch = _unpack(refs, 9, 3, n_cast)
    h_ref, g_ref, w_in_ref, cw_ref, cb_ref, lng_ref, lnb_ref, w_out_ref, ha_ref = ins
    o_ref, sap_ref, glu_s_ref = outs
    ext_ref, cwb_ref, c_ref, sh_ref = scratch
    i = pl.program_id(0)

    @pl.when(i == 0)
    def _():
        for k in range(CONV_A):
            cwb_ref[k] = jnp.broadcast_to(cw_ref[k:k + 1, :], (SUBLANES, D))

    x = h_ref[...]
    xm = _rms(x, g_ref[0:1, :]).astype(_BF)
    a = _dot(xm, w_in_ref[...])
    glu = a[:, :D] * jax.nn.sigmoid(a[:, D:])

    def blocks(v, rows, lanes):
        return v.reshape(rows // SUBLANES, SUBLANES, lanes)

    @pl.when(i < N_PROMPT_TILES)
    def _():
        @pl.when(i % TILES_PER_SEQ == 0)
        def _():
            ext_ref[0:_A_HIST, :] = jnp.zeros((_A_HIST, D), _F32)

        ext_ref[_A_HIST:_A_HIST + TM, :] = glu
        for lb in range(D // _A_LB):
            lanes = slice(lb * _A_LB, (lb + 1) * _A_LB)
            for r in range(1, SUBLANES):
                sh_ref[r - 1] = ext_ref[r:r + _A_SHIFT_ROWS, lanes]
            bias = jnp.broadcast_to(cb_ref[:, lanes], (SUBLANES, _A_LB))[None]
            for rb in range(TM // _A_RB):
                r0 = rb * _A_RB
                acc = jnp.broadcast_to(bias, (_A_RB // SUBLANES, SUBLANES, _A_LB))
                for k in range(CONV_A):
                    r = (_A_FIRST + k) % SUBLANES
                    s = r0 + _A_FIRST + k - r
                    if r == 0:
                        rows = ext_ref[s:s + _A_RB, lanes]
                    else:
                        rows = sh_ref[r - 1, s:s + _A_RB, :]
                    acc = acc + blocks(rows, _A_RB, _A_LB) * cwb_ref[k][:, lanes][None]
                c_ref[r0:r0 + _A_RB, lanes] = acc.reshape(_A_RB, _A_LB)
        sap_ref[0] = ext_ref[TM + _A_FIRST:TM + _A_HIST, :]
        ext_ref[0:_A_HIST, :] = ext_ref[TM:TM + _A_HIST, :]

    @pl.when(i == N_PROMPT_TILES)
    def _():
        glu_s_ref[...] = glu
        ext_ref[0:TM, :] = glu
        bias = jnp.broadcast_to(cb_ref[...], (SUBLANES, D))[None]
        for l in range(DEC_SEQ):
            for rb in range(DEC_BATCH // _A_RB):
                r0 = l * DEC_BATCH + rb * _A_RB
                acc = blocks(ha_ref[r0:r0 + _A_RB, :], _A_RB, D) + bias
                for lp in range(l + 1):
                    s = lp * DEC_BATCH + rb * _A_RB
                    acc = acc + blocks(ext_ref[s:s + _A_RB, :], _A_RB, D) * cwb_ref[CONV_A - 1 + lp - l][None]
                c_ref[r0:r0 + _A_RB, :] = acc.reshape(_A_RB, D)

    c = _ln(c_ref[...], lng_ref[...], lnb_ref[...])
    c = c * jax.nn.sigmoid(c)
    m = _dot(c.astype(_BF), w_out_ref[...])
    o_ref[...] = x + _rms(m, g_ref[1:2, :])
    _run_casts(cast_in, cast_out)


def _mix_a(h, g2, w_in, cw, cb, lng, lnb, w_out, ha, casts):
    return _call(
        _mix_a_body,
        grid=N_TILES,
        in_specs=[_tile_spec(), _const_spec((2, D)), _const_spec((D, 2 * D)),
                  _const_spec((CONV_A, D)), _const_spec((1, D)), _const_spec((1, D)),
                  _const_spec((1, D)), _const_spec((D, D)), _const_spec((TM, D))],
        args=(h, g2, w_in, cw, cb, lng, lnb, w_out, ha),
        out_specs=[_tile_spec(),
                   pl.BlockSpec((1, CONV_A - 1, D),
                                lambda i: (_prompt_tile_index(i) // TILES_PER_SEQ, 0, 0)),
                   pl.BlockSpec((TM, D), lambda i: (0, 0))],
        out_shapes=[jax.ShapeDtypeStruct((ROWS, D), _F32),
                    jax.ShapeDtypeStruct((BATCH, CONV_A - 1, D), _F32),
                    jax.ShapeDtypeStruct((TM, D), _F32)],
        scratch_shapes=[pltpu.VMEM((_A_HIST + TM, D), _F32),
                        pltpu.VMEM((CONV_A, SUBLANES, D), _F32),
                        pltpu.VMEM((TM, D), _F32),
                        pltpu.VMEM((SUBLANES - 1, _A_SHIFT_ROWS, _A_LB), _F32)],
        casts=casts, name="mix_a")


_B_HIST = 8


def _mix_b_body(*refs, n_cast):
    ins, cast_in, outs, cast_out, scratch = _unpack(refs, 6, 3, n_cast)
    h_ref, g_ref, w_in_ref, cw_ref, w_out_ref, st_ref = ins
    o_ref, sbp_ref, z_s_ref = outs
    ext_ref, c_ref = scratch
    i = pl.program_id(0)
    x = h_ref[...]
    xm = _rms(x, g_ref[0:1, :]).astype(_BF)
    p = _dot(xm, w_in_ref[...])
    b_gate = p[:, :D]
    z = p[:, D:2 * D] * p[:, 2 * D:]
    w0 = cw_ref[0:1, :]
    w1 = cw_ref[1:2, :]
    w2 = cw_ref[2:3, :]

    @pl.when(i < N_PROMPT_TILES)
    def _():
        @pl.when(i % TILES_PER_SEQ == 0)
        def _():
            ext_ref[0:_B_HIST, :] = jnp.zeros((_B_HIST, D), _F32)

        ext_ref[_B_HIST:_B_HIST + TM, :] = z
        c_ref[...] = (ext_ref[_B_HIST - 2:_B_HIST - 2 + TM, :] * w0
                      + ext_ref[_B_HIST - 1:_B_HIST - 1 + TM, :] * w1
                      + z * w2)
        sbp_ref[0] = z[TM - (CONV_B - 1):, :]
        ext_ref[0:_B_HIST, :] = ext_ref[TM:TM + _B_HIST, :]

    @pl.when(i == N_PROMPT_TILES)
    def _():
        zs = [st_ref[0], st_ref[1]] + [z[l * DEC_BATCH:(l + 1) * DEC_BATCH, :] for l in range(DEC_SEQ)]
        for l in range(DEC_SEQ):
            c_ref[l * DEC_BATCH:(l + 1) * DEC_BATCH, :] = zs[l] * w0 + zs[l + 1] * w1 + zs[l + 2] * w2
        z_s_ref[...] = z[(DEC_SEQ - (CONV_B - 1)) * DEC_BATCH:, :]

    m = _dot((b_gate * c_ref[...]).astype(_BF), w_out_ref[...])
    o_ref[...] = x + _rms(m, g_ref[1:2, :])
    _run_casts(cast_in, cast_out)


def _mix_b(h, g2, w_in, cw, w_out, st_t, casts):
    n_keep = CONV_B - 1
    return _call(
        _mix_b_body,
        grid=N_TILES,
        in_specs=[_tile_spec(), _const_spec((2, D)), _const_spec((D, 3 * D)),
                  _const_spec((CONV_B, D)), _const_spec((D, D)),
                  _const_spec((n_keep, DEC_BATCH, D))],
        args=(h, g2, w_in, cw, w_out, st_t),
        out_specs=[_tile_spec(),
                   pl.BlockSpec((1, n_keep, D),
                                lambda i: (_prompt_tile_index(i) // TILES_PER_SEQ, 0, 0)),
                   pl.BlockSpec((n_keep * DEC_BATCH, D), lambda i: (0, 0))],
        out_shapes=[jax.ShapeDtypeStruct((ROWS, D), _F32),
                    jax.ShapeDtypeStruct((BATCH, n_keep, D), _F32),
                    jax.ShapeDtypeStruct((n_keep * DEC_BATCH, D), _F32)],
        scratch_shapes=[pltpu.VMEM((_B_HIST + TM, D), _F32),
                        pltpu.VMEM((TM, D), _F32)],
        casts=casts, name="mix_b")


_HD = D // HEADS


def _mix_c_body(*refs, n_cast):
    ins, cast_in, outs, cast_out, scratch = _unpack(refs, 10, 2, n_cast)
    h_ref, g_ref, w_in_ref, lng_ref, lnb_ref, ws_ref, bst_ref, wsv_ref, bsv_ref, w_out_ref = ins
    o_ref, v_s_ref = outs
    (s_ref,) = scratch
    i = pl.program_id(0)
    x = h_ref[...]
    xm = _rms(x, g_ref[0:1, :]).astype(_BF)
    zz = _gelu(_dot(xm, w_in_ref[...]))
    u = zz[:, :D]
    v = _ln(zz[:, D:], lng_ref[...], lnb_ref[...])

    @pl.when(i < N_PROMPT_TILES)
    def _():
        vb = v.astype(_BF)
        n_chunks = TM // CHUNK
        row = lax.broadcasted_iota(jnp.int32, (CHUNK, CHUNK), 0)
        col = lax.broadcasted_iota(jnp.int32, (CHUNK, CHUNK), 1)
        causal = col <= row
        for hd in range(HEADS):
            lanes = slice(hd * _HD, (hd + 1) * _HD)
            wm = jnp.where(causal, ws_ref[hd], 0.0).astype(_BF)
            rhs = jnp.concatenate([vb[n * CHUNK:(n + 1) * CHUNK, lanes] for n in range(n_chunks)], axis=1)
            out = _dot(wm, rhs) + bst_ref[:, hd:hd + 1]
            for n in range(n_chunks):
                s_ref[n * CHUNK:(n + 1) * CHUNK, lanes] = out[:, n * CHUNK:(n + 1) * CHUNK]

    @pl.when(i == N_PROMPT_TILES)
    def _():
        v_s_ref[...] = v
        for l in range(DEC_SEQ):
            acc = jnp.broadcast_to(bsv_ref[l:l + 1, :], (DEC_BATCH, D))
            for lp in range(l + 1):
                acc = acc + v[lp * DEC_BATCH:(lp + 1) * DEC_BATCH, :] * wsv_ref[l, lp:lp + 1, :]
            s_ref[l * DEC_BATCH:(l + 1) * DEC_BATCH, :] = acc

    m = _dot((u * s_ref[...]).astype(_BF), w_out_ref[...])
    o_ref[...] = x + _rms(m, g_ref[1:2, :])
    _run_casts(cast_in, cast_out)


def _mix_c(h, g2, w_in, lng, lnb, ws, bst, wsv, bsv, w_out, casts):
    return _call(
        _mix_c_body,
        grid=N_TILES,
        in_specs=[_tile_spec(), _const_spec((2, D)), _const_spec((D, 2 * D)),
                  _const_spec((1, D)), _const_spec((1, D)),
                  _const_spec((HEADS, CHUNK, CHUNK)), _const_spec((CHUNK, HEADS)),
                  _const_spec((DEC_SEQ, DEC_SEQ, D)), _const_spec((DEC_SEQ, D)),
                  _const_spec((D, D))],
        args=(h, g2, w_in, lng, lnb, ws, bst, wsv, bsv, w_out),
        out_specs=[_tile_spec(), pl.BlockSpec((TM, D), lambda i: (0, 0))],
        out_shapes=[jax.ShapeDtypeStruct((ROWS, D), _F32),
                    jax.ShapeDtypeStruct((TM, D), _F32)],
        scratch_shapes=[pltpu.VMEM((TM, D), _F32)],
        casts=casts, name="mix_c")


_D_ZERO = 8
_D_HIST = 24
_D_ROWS = _D_HIST + TM


def _mix_d_body(*refs, n_cast):
    ins, cast_in, outs, cast_out, scratch = _unpack(refs, 5, 3, n_cast)
    h_ref, g_ref, wg_ref, scale_ref, hp_ref = ins
    o_ref, spp_ref, xm_s_ref = outs
    ext_ref, sa_ref, sb_ref, y_ref = scratch
    i = pl.program_id(0)
    x = h_ref[...]
    xm = _rms(x, g_ref[0:1, :])
    g1, g2, g3 = POOL_GD, 2 * POOL_GD, 3 * POOL_GD

    @pl.when(i == 0)
    def _():
        for ref in (ext_ref, sa_ref, sb_ref):
            ref[0:_D_ZERO, :] = jnp.zeros((_D_ZERO, D), _F32)

    @pl.when(i < N_PROMPT_TILES)
    def _():
        @pl.when(i % TILES_PER_SEQ == 0)
        def _():
            ext_ref[_D_ZERO:_D_HIST, :] = jnp.zeros((_D_HIST - _D_ZERO, D), _F32)

        ext_ref[_D_HIST:_D_ROWS, :] = xm
        lo, hi = _D_ZERO, _D_ROWS
        sa_ref[lo:hi, :] = ext_ref[lo:hi, :] + ext_ref[lo - 1:hi - 1, :]
        sb_ref[lo:hi, g1:] = sa_ref[lo:hi, g1:] + sa_ref[lo - 2:hi - 2, g1:]
        sa_ref[lo:hi, g2:] = sb_ref[lo:hi, g2:] + sb_ref[lo - 4:hi - 4, g2:]
        tots = (sa_ref[_D_HIST:hi, 0:g1], sb_ref[_D_HIST:hi, g1:g2], sa_ref[_D_HIST:hi, g2:g3],
                sa_ref[_D_HIST:hi, g3:] + sa_ref[_D_HIST - 8:hi - 8, g3:])
        pos = (i % TILES_PER_SEQ) * TM + lax.broadcasted_iota(jnp.int32, (TM, 1), 0)
        for gi, w in enumerate(POOL_WINDOWS):
            lanes = slice(gi * POOL_GD, (gi + 1) * POOL_GD)
            cnt = jnp.minimum(w, pos + 1).astype(_F32)
            diff = tots[gi] / cnt - xm[:, lanes]
            y_ref[:, lanes] = _dot(diff.astype(_BF), wg_ref[gi * POOL_GD:(gi + 1) * POOL_GD, :])
        spp_ref[0] = ext_ref[_D_ROWS - POOL_PAD:_D_ROWS, :]
        ext_ref[_D_ZERO:_D_HIST, :] = ext_ref[TM + _D_ZERO:_D_ROWS, :]

    @pl.when(i == N_PROMPT_TILES)
    def _():
        xm_s_ref[...] = xm
        for gi, w in enumerate(POOL_WINDOWS):
            lanes = slice(gi * POOL_GD, (gi + 1) * POOL_GD)
            for l in range(DEC_SEQ):
                rows = slice(l * DEC_BATCH, (l + 1) * DEC_BATCH)
                cur = xm[rows, lanes]
                tot = cur + hp_ref[rows, lanes]
                for j in range(1, min(w - 1, l) + 1):
                    tot = tot + xm[(l - j) * DEC_BATCH:(l - j + 1) * DEC_BATCH, lanes]
                diff = tot / float(w) - cur
                y_ref[rows, lanes] = _dot(diff.astype(_BF), wg_ref[gi * POOL_GD:(gi + 1) * POOL_GD, :])

    m = y_ref[...] * scale_ref[...]
    o_ref[...] = x + _rms(m, g_ref[1:2, :])
    _run_casts(cast_in, cast_out)


def _mix_d(h, g2, wg, scale, hp, casts):
    return _call(
        _mix_d_body,
        grid=N_TILES,
        in_specs=[_tile_spec(), _const_spec((2, D)),
                  _const_spec((len(POOL_WINDOWS) * POOL_GD, POOL_GD)), _const_spec((1, D)),
                  _const_spec((TM, D))],
        args=(h, g2, wg, scale, hp),
        out_specs=[_tile_spec(),
                   pl.BlockSpec((1, POOL_PAD, D),
                                lambda i: (_prompt_tile_index(i) // TILES_PER_SEQ, 0, 0)),
                   pl.BlockSpec((TM, D), lambda i: (0, 0))],
        out_shapes=[jax.ShapeDtypeStruct((ROWS, D), _F32),
                    jax.ShapeDtypeStruct((BATCH, POOL_PAD, D), _F32),
                    jax.ShapeDtypeStruct((TM, D), _F32)],
        scratch_shapes=[pltpu.VMEM((_D_ROWS, D), _F32), pltpu.VMEM((_D_ROWS, D), _F32),
                        pltpu.VMEM((_D_ROWS, D), _F32), pltpu.VMEM((TM, D), _F32)],
        casts=casts, name="mix_d")


_STATE_BB = 16


def _advance_body(st_ref, new_ref, o_ref):
    keep = st_ref.shape[1] - DEC_SEQ
    for b in range(_STATE_BB):
        o_ref[b, 0:keep, :] = st_ref[b, DEC_SEQ:, :]
        o_ref[b, keep:, :] = new_ref[b]


def _advance_state(state, new_rows, name):
    length = state.shape[1]
    return pl.pallas_call(
        _advance_body,
        grid=(DEC_BATCH // _STATE_BB,),
        in_specs=[pl.BlockSpec((_STATE_BB, length, D), lambda i: (i, 0, 0)),
                  pl.BlockSpec((_STATE_BB, DEC_SEQ, D), lambda i: (i, 0, 0))],
        out_specs=pl.BlockSpec((_STATE_BB, length, D), lambda i: (i, 0, 0)),
        out_shape=jax.ShapeDtypeStruct(state.shape, _F32),
        compiler_params=_params(),
        name=name,
    )(state, new_rows)


def _to_pos_major(a):
    return jnp.transpose(a, (1, 0, 2))


def _from_pos_major(a, length):
    return jnp.transpose(a.reshape(length, DEC_BATCH, D), (1, 0, 2))


def kernel(x_prompt, x_sample, state_conv_a, state_conv_b, state_pool, norm_g, ffn_w_in, ffn_w_down,
           a_w_in, a_conv_w, a_conv_b, a_ln_g, a_ln_b, a_w_out, b_w_in, b_conv_w, b_w_out,
           c_w_in, c_ln_g, c_ln_b, c_ws, c_bs, c_w_out, d_w_group, d_scale):
    row = lambda a: a.reshape(1, D)
    ffn_casts = lambda layer, half: _ffn_weight_casts(ffn_w_in, ffn_w_down, layer, half)
    mats = lambda *ws: [_matrix_cast(w) for w in ws]

    xp = x_prompt.reshape(BATCH * SEQ, D)
    xs = _to_pos_major(x_sample).reshape(TM, D)
    ha, hp = _hist(_to_pos_major(state_conv_a), a_conv_w, _to_pos_major(state_pool))

    (h,), (w_in, w_down, a_in, a_out) = _ffn(
        (xp, xs), norm_g[0, 0:2], ffn_w_in[0, 0].astype(_BF), ffn_w_down[0, 0].astype(_BF),
        ffn_casts(0, 1) + mats(a_w_in, a_w_out), split_in=True, name="ffn_0_0")
    (h, sa_p, glu_s), _ = _mix_a(
        h, norm_g[0, 2:4], a_in, a_conv_w, row(a_conv_b), row(a_ln_g), row(a_ln_b), a_out, ha, [])
    (h,), (w_in, w_down) = _ffn((h,), norm_g[0, 4:6], w_in, w_down, ffn_casts(1, 0), name="ffn_0_1")

    (h,), (w_in, w_down, b_in, b_out) = _ffn(
        (h,), norm_g[1, 0:2], w_in, w_down, ffn_casts(1, 1) + mats(b_w_in, b_w_out), name="ffn_1_0")
    (h, sb_p, z_s), _ = _mix_b(h, norm_g[1, 2:4], b_in, b_conv_w, b_out, _to_pos_major(state_conv_b), [])
    (h,), (w_in, w_down) = _ffn((h,), norm_g[1, 4:6], w_in, w_down, ffn_casts(2, 0), name="ffn_1_1")

    (h,), (w_in, w_down, c_in, c_out) = _ffn(
        (h,), norm_g[2, 0:2], w_in, w_down, ffn_casts(2, 1) + mats(c_w_in, c_w_out), name="ffn_2_0")
    wsv = jnp.repeat(jnp.transpose(c_ws[:, :DEC_SEQ, :DEC_SEQ], (1, 2, 0)), _HD, axis=-1)
    bsv = jnp.repeat(jnp.transpose(c_bs[:, :DEC_SEQ]), _HD, axis=-1)
    (h, v_s), _ = _mix_c(h, norm_g[2, 2:4], c_in, row(c_ln_g), row(c_ln_b), c_ws, jnp.transpose(c_bs),
                         wsv, bsv, c_out, [])
    (h,), (w_in, w_down) = _ffn((h,), norm_g[2, 4:6], w_in, w_down, ffn_casts(3, 0), name="ffn_2_1")

    (h,), (w_in, w_down, d_wg) = _ffn(
        (h,), norm_g[3, 0:2], w_in, w_down,
        ffn_casts(3, 1) + mats(d_w_group.reshape(len(POOL_WINDOWS) * POOL_GD, POOL_GD)), name="ffn_3_0")
    (h, sp_p, xm_s), _ = _mix_d(h, norm_g[3, 2:4], d_wg, row(d_scale), hp, [])
    (yp, ys), _ = _ffn((h,), norm_g[3, 4:6], w_in, w_down, split_out=True, name="ffn_3_1")

    y_prompt = yp.reshape(BATCH, SEQ, D)
    y_sample = _from_pos_major(ys, DEC_SEQ)
    sa_s = _advance_state(state_conv_a, _from_pos_major(glu_s, DEC_SEQ), "state_a")
    sb_s = _from_pos_major(z_s, CONV_B - 1)
    sc_s = _from_pos_major(v_s, DEC_SEQ)
    sp_s = _advance_state(state_pool, _from_pos_major(xm_s, DEC_SEQ), "state_pool")
    return (y_prompt, y_sample, sa_p, sa_s, sb_p, sb_s, sc_s, sp_p, sp_s)
```

```python
import functools
from typing import Any, NamedTuple

import jax
import jax.numpy as jnp
from jax import lax
from jax.experimental import pallas as pl
from jax.experimental.pallas import tpu as pltpu

D = 1024
D_FF = 2816
DEPTH = 4
EPS = 1e-6
FFN_HALF = 0.5
CONV_A = 31
CONV_B = 3
CHUNK = 128
HEADS = 8
POOL_WINDOWS = (2, 4, 8, 16)
POOL_GD = D // len(POOL_WINDOWS)
POOL_PAD = 15

BATCH = 8
SEQ = 2048
DEC_BATCH = 128
DEC_SEQ = 4

TM = 512
TILES_PER_SEQ = SEQ // TM
N_PROMPT_TILES = BATCH * SEQ // TM
N_TILES = N_PROMPT_TILES + 1
ROWS = N_TILES * TM
FF_CHUNK = 256
SUBLANES = 8
LANES = 128

FFN_CAST_STEPS = 22
MIX_CAST_STEPS = 8

_BF = jnp.bfloat16
_F32 = jnp.float32
_VMEM_LIMIT = 60 * 1024 * 1024


def _dot(a, b):
    return jnp.dot(a, b, preferred_element_type=_F32)


def _rms(x, g):
    return x * lax.rsqrt(jnp.mean(x * x, axis=-1, keepdims=True) + EPS) * g


def _ln(x, g, b):
    xc = x - jnp.mean(x, axis=-1, keepdims=True)
    return xc * lax.rsqrt(jnp.mean(xc * xc, axis=-1, keepdims=True) + EPS) * g + b


def _gelu(x):
    return 0.5 * x * (1.0 + lax.erf(x * (2.0 ** -0.5)))


def _const_spec(shape):
    zeros = (0,) * len(shape)
    return pl.BlockSpec(shape, lambda i: zeros, pipeline_mode=pl.Buffered(1))


def _tile_spec():
    return pl.BlockSpec((TM, D), lambda i: (i, 0))


def _prompt_tile_index(i):
    return jnp.minimum(i, N_PROMPT_TILES - 1)


def _params():
    return pltpu.CompilerParams(dimension_semantics=("arbitrary",),
                                vmem_limit_bytes=_VMEM_LIMIT)


class _Cast(NamedTuple):
    src: Any
    in_spec: Any
    out_spec: Any
    out_shape: Any


def _ffn_weight_casts(ffn_w_in, ffn_w_down, layer, half):
    n = FFN_CAST_STEPS
    cb = 2 * D_FF // n
    rb = D_FF // n
    step = lambda i: jnp.minimum(i, n - 1)
    return [
        _Cast(ffn_w_in,
              pl.BlockSpec((None, None, D, cb), lambda i: (layer, half, 0, step(i))),
              pl.BlockSpec((D, cb), lambda i: (0, step(i))),
              jax.ShapeDtypeStruct((D, 2 * D_FF), _BF)),
        _Cast(ffn_w_down,
              pl.BlockSpec((None, None, rb, D), lambda i: (layer, half, step(i), 0)),
              pl.BlockSpec((rb, D), lambda i: (step(i), 0)),
              jax.ShapeDtypeStruct((D_FF, D), _BF)),
    ]


def _matrix_cast(w):
    n = MIX_CAST_STEPS
    rows, cols = w.shape
    rb = rows // n
    step = lambda i: jnp.minimum(i, n - 1)
    return _Cast(w,
                 pl.BlockSpec((rb, cols), lambda i: (step(i), 0)),
                 pl.BlockSpec((rb, cols), lambda i: (step(i), 0)),
                 jax.ShapeDtypeStruct((rows, cols), _BF))


def _run_casts(in_refs, out_refs):
    for src_ref, dst_ref in zip(in_refs, out_refs):
        dst_ref[...] = src_ref[...].astype(_BF)


def _unpack(refs, n_in, n_out, n_cast):
    refs = list(refs)
    o0 = n_in + n_cast
    s0 = o0 + n_out + n_cast
    return refs[:n_in], refs[n_in:o0], refs[o0:o0 + n_out], refs[o0 + n_out:s0], refs[s0:]


def _call(body, *, grid, in_specs, args, out_specs, out_shapes, casts, scratch_shapes=(), name):
    outs = pl.pallas_call(
        functools.partial(body, n_cast=len(casts)),
        grid=(grid,),
        in_specs=list(in_specs) + [c.in_spec for c in casts],
        out_specs=list(out_specs) + [c.out_spec for c in casts],
        out_shape=list(out_shapes) + [c.out_shape for c in casts],
        scratch_shapes=list(scratch_shapes),
        compiler_params=_params(),
        name=name,
    )(*args, *[c.src for c in casts])
    n_out = len(out_specs)
    return outs[:n_out], outs[n_out:]


def _ffn_math(x, g_ref, w_in_ref, w_down_ref):
    xn = _rms(x, g_ref[0:1, :]).astype(_BF)
    acc = None
    for c in range(D_FF // FF_CHUNK):
        c0 = c * FF_CHUNK
        gate = _dot(xn, w_in_ref[:, c0:c0 + FF_CHUNK])
        up = _dot(xn, w_in_ref[:, D_FF + c0:D_FF + c0 + FF_CHUNK])
        act = (gate * jax.nn.sigmoid(gate) * up).astype(_BF)
        d = _dot(act, w_down_ref[c0:c0 + FF_CHUNK, :])
        acc = d if acc is None else acc + d
    return x + FFN_HALF * _rms(acc, g_ref[1:2, :])


def _ffn_body(*refs, split_in, split_out, n_cast):
    n_x = 2 if split_in else 1
    ins, cast_in, outs, cast_out, _ = _unpack(refs, n_x + 3, 2 if split_out else 1, n_cast)
    x_refs = ins[:n_x]
    g_ref, w_in_ref, w_down_ref = ins[n_x:]
    i = pl.program_id(0)
    if split_in:
        x = jnp.where(i < N_PROMPT_TILES, x_refs[0][...], x_refs[1][...])
    else:
        x = x_refs[0][...]
    y = _ffn_math(x, g_ref, w_in_ref, w_down_ref)
    if split_out:
        @pl.when(i < N_PROMPT_TILES)
        def _():
            outs[0][...] = y

        @pl.when(i == N_PROMPT_TILES)
        def _():
            outs[1][...] = y
    else:
        outs[0][...] = y
    _run_casts(cast_in, cast_out)


def _ffn(xs, g2, w_in, w_down, casts=(), *, split_in=False, split_out=False, name):
    split_specs = [pl.BlockSpec((TM, D), lambda i: (_prompt_tile_index(i), 0)),
                   pl.BlockSpec((TM, D), lambda i: (0, 0))]
    split_shapes = [jax.ShapeDtypeStruct((N_PROMPT_TILES * TM, D), _F32),
                    jax.ShapeDtypeStruct((TM, D), _F32)]
    return _call(
        functools.partial(_ffn_body, split_in=split_in, split_out=split_out),
        grid=N_TILES,
        in_specs=(split_specs if split_in else [_tile_spec()])
        + [_const_spec((2, D)), _const_spec((D, 2 * D_FF)), _const_spec((D_FF, D))],
        args=(*xs, g2, w_in, w_down),
        out_specs=split_specs if split_out else [_tile_spec()],
        out_shapes=split_shapes if split_out else [jax.ShapeDtypeStruct((ROWS, D), _F32)],
        casts=list(casts), name=name)


_HIST_BB = 32


def _hist_body(sta_ref, cw_ref, stp_ref, ha_ref, hp_ref):
    for l in range(DEC_SEQ):
        acc = jnp.zeros((_HIST_BB, D), _F32)
        for j in range(l, CONV_A - 1):
            acc = acc + sta_ref[j] * cw_ref[j - l:j - l + 1, :]
        ha_ref[l] = acc
        for gi, w in enumerate(POOL_WINDOWS):
            lanes = slice(gi * POOL_GD, (gi + 1) * POOL_GD)
            tot = jnp.zeros((_HIST_BB, POOL_GD), _F32)
            for j in range(l + 1, w):
                tot = tot + stp_ref[POOL_PAD + l - j, :, lanes]
            hp_ref[l, :, lanes] = tot


def _hist(st_a_t, cw, st_p_t):
    out = jax.ShapeDtypeStruct((DEC_SEQ, DEC_BATCH, D), _F32)
    ha, hp = pl.pallas_call(
        _hist_body,
        grid=(DEC_BATCH // _HIST_BB,),
        in_specs=[pl.BlockSpec((CONV_A - 1, _HIST_BB, D), lambda i: (0, i, 0)),
                  pl.BlockSpec((CONV_A, D), lambda i: (0, 0)),
                  pl.BlockSpec((POOL_PAD, _HIST_BB, D), lambda i: (0, i, 0))],
        out_specs=[pl.BlockSpec((DEC_SEQ, _HIST_BB, D), lambda i: (0, i, 0)),
                   pl.BlockSpec((DEC_SEQ, _HIST_BB, D), lambda i: (0, i, 0))],
        out_shape=[out, out],
        compiler_params=_params(),
        name="sample_hist",
    )(st_a_t, cw, st_p_t)
    return ha.reshape(TM, D), hp.reshape(TM, D)


_A_HIST = 32
_A_FIRST = _A_HIST - (CONV_A - 1)
_A_LB = 256
_A_RB = 64
_A_SHIFT_ROWS = 536


def _mix_a_body(*refs, n_cast):
    ins, cast_in, outs, cast_out, scratch = _unpack(refs, 9, 3, n_cast)
    h_ref, g_ref, w_in_ref, cw_ref, cb_ref, lng_ref, lnb_ref, w_out_ref, ha_ref = ins
    o_ref, sap_ref, glu_s_ref = outs
    ext_ref, cwb_ref, c_ref, sh_ref = scratch
    i = pl.program_id(0)

    @pl.when(i == 0)
    def _():
        for k in range(CONV_A):
            cwb_ref[k] = jnp.broadcast_to(cw_ref[k:k + 1, :], (SUBLANES, D))

    x = h_ref[...]
    xm = _rms(x, g_ref[0:1, :]).astype(_BF)
    a = _dot(xm, w_in_ref[...])
    glu = a[:, :D] * jax.nn.sigmoid(a[:, D:])

    def blocks(v, rows, lanes):
        return v.reshape(rows // SUBLANES, SUBLANES, lanes)

    @pl.when(i < N_PROMPT_TILES)
    def _():
        @pl.when(i % TILES_PER_SEQ == 0)
        def _():
            ext_ref[0:_A_HIST, :] = jnp.zeros((_A_HIST, D), _F32)

        ext_ref[_A_HIST:_A_HIST + TM, :] = glu
        for lb in range(D // _A_LB):
            lanes = slice(lb * _A_LB, (lb + 1) * _A_LB)
            for r in range(1, SUBLANES):
                sh_ref[r - 1] = ext_ref[r:r + _A_SHIFT_ROWS, lanes]
            bias = jnp.broadcast_to(cb_ref[:, lanes], (SUBLANES, _A_LB))[None]
            for rb in range(TM // _A_RB):
                r0 = rb * _A_RB
                acc = jnp.broadcast_to(bias, (_A_RB // SUBLANES, SUBLANES, _A_LB))
                for k in range(CONV_A):
                    r = (_A_FIRST + k) % SUBLANES
                    s = r0 + _A_FIRST + k - r
                    if r == 0:
                        rows = ext_ref[s:s + _A_RB, lanes]
                    else:
                        rows = sh_ref[r - 1, s:s + _A_RB, :]
                    acc = acc + blocks(rows, _A_RB, _A_LB) * cwb_ref[k][:, lanes][None]
                c_ref[r0:r0 + _A_RB, lanes] = acc.reshape(_A_RB, _A_LB)
        sap_ref[0] = ext_ref[TM + _A_FIRST:TM + _A_HIST, :]
        ext_ref[0:_A_HIST, :] = ext_ref[TM:TM + _A_HIST, :]

    @pl.when(i == N_PROMPT_TILES)
    def _():
        glu_s_ref[...] = glu
        ext_ref[0:TM, :] = glu
        bias = jnp.broadcast_to(cb_ref[...], (SUBLANES, D))[None]
        for l in range(DEC_SEQ):
            for rb in range(DEC_BATCH // _A_RB):
                r0 = l * DEC_BATCH + rb * _A_RB
                acc = blocks(ha_ref[r0:r0 + _A_RB, :], _A_RB, D) + bias
                for lp in range(l + 1):
                    s = lp * DEC_BATCH + rb * _A_RB
                    acc = acc + blocks(ext_ref[s:s + _A_RB, :], _A_RB, D) * cwb_ref[CONV_A - 1 + lp - l][None]
                c_ref[r0:r0 + _A_RB, :] = acc.reshape(_A_RB, D)

    c = _ln(c_ref[...], lng_ref[...], lnb_ref[...])
    c = c * jax.nn.sigmoid(c)
    m = _dot(c.astype(_BF), w_out_ref[...])
    o_ref[...] = x + _rms(m, g_ref[1:2, :])
    _run_casts(cast_in, cast_out)


def _mix_a(h, g2, w_in, cw, cb, lng, lnb, w_out, ha, casts):
    return _call(
        _mix_a_body,
        grid=N_TILES,
        in_specs=[_tile_spec(), _const_spec((2, D)), _const_spec((D, 2 * D)),
                  _const_spec((CONV_A, D)), _const_spec((1, D)), _const_spec((1, D)),
                  _const_spec((1, D)), _const_spec((D, D)), _const_spec((TM, D))],
        args=(h, g2, w_in, cw, cb, lng, lnb, w_out, ha),
        out_specs=[_tile_spec(),
                   pl.BlockSpec((1, CONV_A - 1, D),
                                lambda i: (_prompt_tile_index(i) // TILES_PER_SEQ, 0, 0)),
                   pl.BlockSpec((TM, D), lambda i: (0, 0))],
        out_shapes=[jax.ShapeDtypeStruct((ROWS, D), _F32),
                    jax.ShapeDtypeStruct((BATCH, CONV_A - 1, D), _F32),
                    jax.ShapeDtypeStruct((TM, D), _F32)],
        scratch_shapes=[pltpu.VMEM((_A_HIST + TM, D), _F32),
                        pltpu.VMEM((CONV_A, SUBLANES, D), _F32),
                        pltpu.VMEM((TM, D), _F32),
                        pltpu.VMEM((SUBLANES - 1, _A_SHIFT_ROWS, _A_LB), _F32)],
        casts=casts, name="mix_a")


_B_HIST = 8


def _mix_b_body(*refs, n_cast):
    ins, cast_in, outs, cast_out, scratch = _unpack(refs, 6, 3, n_cast)
    h_ref, g_ref, w_in_ref, cw_ref, w_out_ref, st_ref = ins
    o_ref, sbp_ref, z_s_ref = outs
    ext_ref, c_ref = scratch
    i = pl.program_id(0)
    x = h_ref[...]
    xm = _rms(x, g_ref[0:1, :]).astype(_BF)
    p = _dot(xm, w_in_ref[...])
    b_gate = p[:, :D]
    z = p[:, D:2 * D] * p[:, 2 * D:]
    w0 = cw_ref[0:1, :]
    w1 = cw_ref[1:2, :]
    w2 = cw_ref[2:3, :]

    @pl.when(i < N_PROMPT_TILES)
    def _():
        @pl.when(i % TILES_PER_SEQ == 0)
        def _():
            ext_ref[0:_B_HIST, :] = jnp.zeros((_B_HIST, D), _F32)

        ext_ref[_B_HIST:_B_HIST + TM, :] = z
        c_ref[...] = (ext_ref[_B_HIST - 2:_B_HIST - 2 + TM, :] * w0
                      + ext_ref[_B_HIST - 1:_B_HIST - 1 + TM, :] * w1
                      + z * w2)
        sbp_ref[0] = z[TM - (CONV_B - 1):, :]
        ext_ref[0:_B_HIST, :] = ext_ref[TM:TM + _B_HIST, :]

    @pl.when(i == N_PROMPT_TILES)
    def _():
        zs = [st_ref[0], st_ref[1]] + [z[l * DEC_BATCH:(l + 1) * DEC_BATCH, :] for l in range(DEC_SEQ)]
        for l in range(DEC_SEQ):
            c_ref[l * DEC_BATCH:(l + 1) * DEC_BATCH, :] = zs[l] * w0 + zs[l + 1] * w1 + zs[l + 2] * w2
        z_s_ref[...] = z[(DEC_SEQ - (CONV_B - 1)) * DEC_BATCH:, :]

    m = _dot((b_gate * c_ref[...]).astype(_BF), w_out_ref[...])
    o_ref[...] = x + _rms(m, g_ref[1:2, :])
    _run_casts(cast_in, cast_out)


def _mix_b(h, g2, w_in, cw, w_out, st_t, casts):
    n_keep = CONV_B - 1
    return _call(
        _mix_b_body,
        grid=N_TILES,
        in_specs=[_tile_spec(), _const_spec((2, D)), _const_spec((D, 3 * D)),
                  _const_spec((CONV_B, D)), _const_spec((D, D)),
                  _const_spec((n_keep, DEC_BATCH, D))],
        args=(h, g2, w_in, cw, w_out, st_t),
        out_specs=[_tile_spec(),
                   pl.BlockSpec((1, n_keep, D),
                                lambda i: (_prompt_tile_index(i) // TILES_PER_SEQ, 0, 0)),
                   pl.BlockSpec((n_keep * DEC_BATCH, D), lambda i: (0, 0))],
        out_shapes=[jax.ShapeDtypeStruct((ROWS, D), _F32),
                    jax.ShapeDtypeStruct((BATCH, n_keep, D), _F32),
                    jax.ShapeDtypeStruct((n_keep * DEC_BATCH, D), _F32)],
        scratch_shapes=[pltpu.VMEM((_B_HIST + TM, D), _F32),
                        pltpu.VMEM((TM, D), _F32)],
        casts=casts, name="mix_b")


_HD = D // HEADS


def _mix_c_body(*refs, n_cast):
    ins, cast_in, outs, cast_out, scratch = _unpack(refs, 10, 2, n_cast)
    h_ref, g_ref, w_in_ref, lng_ref, lnb_ref, ws_ref, bst_ref, wsv_ref, bsv_ref, w_out_ref = ins
    o_ref, v_s_ref = outs
    (s_ref,) = scratch
    i = pl.program_id(0)
    x = h_ref[...]
    xm = _rms(x, g_ref[0:1, :]).astype(_BF)
    zz = _gelu(_dot(xm, w_in_ref[...]))
    u = zz[:, :D]
    v = _ln(zz[:, D:], lng_ref[...], lnb_ref[...])

    @pl.when(i < N_PROMPT_TILES)
    def _():
        vb = v.astype(_BF)
        n_chunks = TM // CHUNK
        row = lax.broadcasted_iota(jnp.int32, (CHUNK, CHUNK), 0)
        col = lax.broadcasted_iota(jnp.int32, (CHUNK, CHUNK), 1)
        causal = col <= row
        for hd in range(HEADS):
            lanes = slice(hd * _HD, (hd + 1) * _HD)
            wm = jnp.where(causal, ws_ref[hd], 0.0).astype(_BF)
            rhs = jnp.concatenate([vb[n * CHUNK:(n + 1) * CHUNK, lanes] for n in range(n_chunks)], axis=1)
            out = _dot(wm, rhs) + bst_ref[:, hd:hd + 1]
            for n in range(n_chunks):
                s_ref[n * CHUNK:(n + 1) * CHUNK, lanes] = out[:, n * CHUNK:(n + 1) * CHUNK]

    @pl.when(i == N_PROMPT_TILES)
    def _():
        v_s_ref[...] = v
        for l in range(DEC_SEQ):
            acc = jnp.broadcast_to(bsv_ref[l:l + 1, :], (DEC_BATCH, D))
            for lp in range(l + 1):
                acc = acc + v[lp * DEC_BATCH:(lp + 1) * DEC_BATCH, :] * wsv_ref[l, lp:lp + 1, :]
            s_ref[l * DEC_BATCH:(l + 1) * DEC_BATCH, :] = acc

    m = _dot((u * s_ref[...]).astype(_BF), w_out_ref[...])
    o_ref[...] = x + _rms(m, g_ref[1:2, :])
    _run_casts(cast_in, cast_out)


def _mix_c(h, g2, w_in, lng, lnb, ws, bst, wsv, bsv, w_out, casts):
    return _call(
        _mix_c_body,
        grid=N_TILES,
        in_specs=[_tile_spec(), _const_spec((2, D)), _const_spec((D, 2 * D)),
                  _const_spec((1, D)), _const_spec((1, D)),
                  _const_spec((HEADS, CHUNK, CHUNK)), _const_spec((CHUNK, HEADS)),
                  _const_spec((DEC_SEQ, DEC_SEQ, D)), _const_spec((DEC_SEQ, D)),
                  _const_spec((D, D))],
        args=(h, g2, w_in, lng, lnb, ws, bst, wsv, bsv, w_out),
        out_specs=[_tile_spec(), pl.BlockSpec((TM, D), lambda i: (0, 0))],
        out_shapes=[jax.ShapeDtypeStruct((ROWS, D), _F32),
                    jax.ShapeDtypeStruct((TM, D), _F32)],
        scratch_shapes=[pltpu.VMEM((TM, D), _F32)],
        casts=casts, name="mix_c")


_D_ZERO = 8
_D_HIST = 24
_D_ROWS = _D_HIST + TM


def _mix_d_body(*refs, n_cast):
    ins, cast_in, outs, cast_out, scratch = _unpack(refs, 5, 3, n_cast)
    h_ref, g_ref, wg_ref, scale_ref, hp_ref = ins
    o_ref, spp_ref, xm_s_ref = outs
    ext_ref, sa_ref, sb_ref, y_ref = scratch
    i = pl.program_id(0)
    x = h_ref[...]
    xm = _rms(x, g_ref[0:1, :])
    g1, g2, g3 = POOL_GD, 2 * POOL_GD, 3 * POOL_GD

    @pl.when(i == 0)
    def _():
        for ref in (ext_ref, sa_ref, sb_ref):
            ref[0:_D_ZERO, :] = jnp.zeros((_D_ZERO, D), _F32)

    @pl.when(i < N_PROMPT_TILES)
    def _():
        @pl.when(i % TILES_PER_SEQ == 0)
        def _():
            ext_ref[_D_ZERO:_D_HIST, :] = jnp.zeros((_D_HIST - _D_ZERO, D), _F32)

        ext_ref[_D_HIST:_D_ROWS, :] = xm
        lo, hi = _D_ZERO, _D_ROWS
        sa_ref[lo:hi, :] = ext_ref[lo:hi, :] + ext_ref[lo - 1:hi - 1, :]
        sb_ref[lo:hi, g1:] = sa_ref[lo:hi, g1:] + sa_ref[lo - 2:hi - 2, g1:]
        sa_ref[lo:hi, g2:] = sb_ref[lo:hi, g2:] + sb_ref[lo - 4:hi - 4, g2:]
        tots = (sa_ref[_D_HIST:hi, 0:g1], sb_ref[_D_HIST:hi, g1:g2], sa_ref[_D_HIST:hi, g2:g3],
                sa_ref[_D_HIST:hi, g3:] + sa_ref[_D_HIST - 8:hi - 8, g3:])
        pos = (i % TILES_PER_SEQ) * TM + lax.broadcasted_iota(jnp.int32, (TM, 1), 0)
        for gi, w in enumerate(POOL_WINDOWS):
            lanes = slice(gi * POOL_GD, (gi + 1) * POOL_GD)
            cnt = jnp.minimum(w, pos + 1).astype(_F32)
            diff = tots[gi] / cnt - xm[:, lanes]
            y_ref[:, lanes] = _dot(diff.astype(_BF), wg_ref[gi * POOL_GD:(gi + 1) * POOL_GD, :])
        spp_ref[0] = ext_ref[_D_ROWS - POOL_PAD:_D_ROWS, :]
        ext_ref[_D_ZERO:_D_HIST, :] = ext_ref[TM + _D_ZERO:_D_ROWS, :]

    @pl.when(i == N_PROMPT_TILES)
    def _():
        xm_s_ref[...] = xm
        for gi, w in enumerate(POOL_WINDOWS):
            lanes = slice(gi * POOL_GD, (gi + 1) * POOL_GD)
            for l in range(DEC_SEQ):
                rows = slice(l * DEC_BATCH, (l + 1) * DEC_BATCH)
                cur = xm[rows, lanes]
                tot = cur + hp_ref[rows, lanes]
                for j in range(1, min(w - 1, l) + 1):
                    tot = tot + xm[(l - j) * DEC_BATCH:(l - j + 1) * DEC_BATCH, lanes]
                diff = tot / float(w) - cur
                y_ref[rows, lanes] = _dot(diff.astype(_BF), wg_ref[gi * POOL_GD:(gi + 1) * POOL_GD, :])

    m = y_ref[...] * scale_ref[...]
    o_ref[...] = x + _rms(m, g_ref[1:2, :])
    _run_casts(cast_in, cast_out)


def _mix_d(h, g2, wg, scale, hp, casts):
    return _call(
        _mix_d_body,
        grid=N_TILES,
        in_specs=[_tile_spec(), _const_spec((2, D)),
                  _const_spec((len(POOL_WINDOWS) * POOL_GD, POOL_GD)), _const_spec((1, D)),
                  _const_spec((TM, D))],
        args=(h, g2, wg, scale, hp),
        out_specs=[_tile_spec(),
                   pl.BlockSpec((1, POOL_PAD, D),
                                lambda i: (_prompt_tile_index(i) // TILES_PER_SEQ, 0, 0)),
                   pl.BlockSpec((TM, D), lambda i: (0, 0))],
        out_shapes=[jax.ShapeDtypeStruct((ROWS, D), _F32),
                    jax.ShapeDtypeStruct((BATCH, POOL_PAD, D), _F32),
                    jax.ShapeDtypeStruct((TM, D), _F32)],
        scratch_shapes=[pltpu.VMEM((_D_ROWS, D), _F32), pltpu.VMEM((_D_ROWS, D), _F32),
                        pltpu.VMEM((_D_ROWS, D), _F32), pltpu.VMEM((TM, D), _F32)],
        casts=casts, name="mix_d")


def _to_pos_major(a):
    return jnp.transpose(a, (1, 0, 2))


def _from_pos_major(a, length):
    return jnp.transpose(a.reshape(length, DEC_BATCH, D), (1, 0, 2))


def kernel(x_prompt, x_sample, state_conv_a, state_conv_b, state_pool, norm_g, ffn_w_in, ffn_w_down,
           a_w_in, a_conv_w, a_conv_b, a_ln_g, a_ln_b, a_w_out, b_w_in, b_conv_w, b_w_out,
           c_w_in, c_ln_g, c_ln_b, c_ws, c_bs, c_w_out, d_w_group, d_scale):
    row = lambda a: a.reshape(1, D)
    ffn_casts = lambda layer, half: _ffn_weight_casts(ffn_w_in, ffn_w_down, layer, half)
    mats = lambda *ws: [_matrix_cast(w) for w in ws]

    xp = x_prompt.reshape(BATCH * SEQ, D)
    xs = _to_pos_major(x_sample).reshape(TM, D)
    ha, hp = _hist(_to_pos_major(state_conv_a), a_conv_w, _to_pos_major(state_pool))

    (h,), (w_in, w_down, a_in, a_out) = _ffn(
        (xp, xs), norm_g[0, 0:2], ffn_w_in[0, 0].astype(_BF), ffn_w_down[0, 0].astype(_BF),
        ffn_casts(0, 1) + mats(a_w_in, a_w_out), split_in=True, name="ffn_0_0")
    (h, sa_p, glu_s), _ = _mix_a(
        h, norm_g[0, 2:4], a_in, a_conv_w, row(a_conv_b), row(a_ln_g), row(a_ln_b), a_out, ha, [])
    (h,), (w_in, w_down) = _ffn((h,), norm_g[0, 4:6], w_in, w_down, ffn_casts(1, 0), name="ffn_0_1")

    (h,), (w_in, w_down, b_in, b_out) = _ffn(
        (h,), norm_g[1, 0:2], w_in, w_down, ffn_casts(1, 1) + mats(b_w_in, b_w_out), name="ffn_1_0")
    (h, sb_p, z_s), _ = _mix_b(h, norm_g[1, 2:4], b_in, b_conv_w, b_out, _to_pos_major(state_conv_b), [])
    (h,), (w_in, w_down) = _ffn((h,), norm_g[1, 4:6], w_in, w_down, ffn_casts(2, 0), name="ffn_1_1")

    (h,), (w_in, w_down, c_in, c_out) = _ffn(
        (h,), norm_g[2, 0:2], w_in, w_down, ffn_casts(2, 1) + mats(c_w_in, c_w_out), name="ffn_2_0")
    wsv = jnp.repeat(jnp.transpose(c_ws[:, :DEC_SEQ, :DEC_SEQ], (1, 2, 0)), _HD, axis=-1)
    bsv = jnp.repeat(jnp.transpose(c_bs[:, :DEC_SEQ]), _HD, axis=-1)
    (h, v_s), _ = _mix_c(h, norm_g[2, 2:4], c_in, row(c_ln_g), row(c_ln_b), c_ws, jnp.transpose(c_bs),
                         wsv, bsv, c_out, [])
    (h,), (w_in, w_down) = _ffn((h,), norm_g[2, 4:6], w_in, w_down, ffn_casts(3, 0), name="ffn_2_1")

    (h,), (w_in, w_down, d_wg) = _ffn(
        (h,), norm_g[3, 0:2], w_in, w_down,
        ffn_casts(3, 1) + mats(d_w_group.reshape(len(POOL_WINDOWS) * POOL_GD, POOL_GD)), name="ffn_3_0")
    (h, sp_p, xm_s), _ = _mix_d(h, norm_g[3, 2:4], d_wg, row(d_scale), hp, [])
    (yp, ys), _ = _ffn((h,), norm_g[3, 4:6], w_in, w_down, split_out=True, name="ffn_3_1")

    y_prompt = yp.reshape(BATCH, SEQ, D)
    y_sample = _from_pos_major(ys, DEC_SEQ)
    sa_s = jnp.concatenate([state_conv_a[:, DEC_SEQ:], _from_pos_major(glu_s, DEC_SEQ)], axis=1)
    sb_s = _from_pos_major(z_s, CONV_B - 1)
    sc_s = _from_pos_major(v_s, DEC_SEQ)
    sp_s = jnp.concatenate([state_pool[:, DEC_SEQ:], _from_pos_major(xm_s, DEC_SEQ)], axis=1)
    return (y_prompt, y_sample, sa_p, sa_s, sb_p, sb_s, sc_s, sp_p, sp_s)
```

```python
import functools
from typing import Any, NamedTuple

import jax
import jax.numpy as jnp
from jax import lax
from jax.experimental import pallas as pl
from jax.experimental.pallas import tpu as pltpu

D = 1024
D_FF = 2816
DEPTH = 4
EPS = 1e-6
FFN_HALF = 0.5
CONV_A = 31
CONV_B = 3
CHUNK = 128
HEADS = 8
POOL_WINDOWS = (2, 4, 8, 16)
POOL_GD = D // len(POOL_WINDOWS)
POOL_PAD = 15

BATCH = 8
SEQ = 2048
DEC_BATCH = 128
DEC_SEQ = 4

TM = 512
TILES_PER_SEQ = SEQ // TM
N_PROMPT_TILES = BATCH * SEQ // TM
N_TILES = N_PROMPT_TILES + 1
ROWS = N_TILES * TM
FF_CHUNK = 256
SUBLANES = 8
LANES = 128

FFN_CAST_STEPS = 22
MIX_CAST_STEPS = 8

_BF = jnp.bfloat16
_F32 = jnp.float32
_VMEM_LIMIT = 60 * 1024 * 1024


def _dot(a, b):
    return jnp.dot(a, b, preferred_element_type=_F32)


def _rms(x, g):
    return x * lax.rsqrt(jnp.mean(x * x, axis=-1, keepdims=True) + EPS) * g


def _ln(x, g, b):
    xc = x - jnp.mean(x, axis=-1, keepdims=True)
    return xc * lax.rsqrt(jnp.mean(xc * xc, axis=-1, keepdims=True) + EPS) * g + b


def _gelu(x):
    return 0.5 * x * (1.0 + lax.erf(x * (2.0 ** -0.5)))


def _const_spec(shape):
    zeros = (0,) * len(shape)
    return pl.BlockSpec(shape, lambda i: zeros, pipeline_mode=pl.Buffered(1))


def _tile_spec():
    return pl.BlockSpec((TM, D), lambda i: (i, 0))


def _prompt_tile_index(i):
    return jnp.minimum(i, N_PROMPT_TILES - 1)


def _params():
    return pltpu.CompilerParams(dimension_semantics=("arbitrary",),
                                vmem_limit_bytes=_VMEM_LIMIT)


class _Cast(NamedTuple):
    src: Any
    in_spec: Any
    out_spec: Any
    out_shape: Any


def _ffn_weight_casts(ffn_w_in, ffn_w_down, layer, half):
    n = FFN_CAST_STEPS
    cb = 2 * D_FF // n
    rb = D_FF // n
    step = lambda i: jnp.minimum(i, n - 1)
    return [
        _Cast(ffn_w_in,
              pl.BlockSpec((None, None, D, cb), lambda i: (layer, half, 0, step(i))),
              pl.BlockSpec((D, cb), lambda i: (0, step(i))),
              jax.ShapeDtypeStruct((D, 2 * D_FF), _BF)),
        _Cast(ffn_w_down,
              pl.BlockSpec((None, None, rb, D), lambda i: (layer, half, step(i), 0)),
              pl.BlockSpec((rb, D), lambda i: (step(i), 0)),
              jax.ShapeDtypeStruct((D_FF, D), _BF)),
    ]


def _matrix_cast(w):
    n = MIX_CAST_STEPS
    rows, cols = w.shape
    rb = rows // n
    step = lambda i: jnp.minimum(i, n - 1)
    return _Cast(w,
                 pl.BlockSpec((rb, cols), lambda i: (step(i), 0)),
                 pl.BlockSpec((rb, cols), lambda i: (step(i), 0)),
                 jax.ShapeDtypeStruct((rows, cols), _BF))


def _run_casts(in_refs, out_refs):
    for src_ref, dst_ref in zip(in_refs, out_refs):
        dst_ref[...] = src_ref[...].astype(_BF)


def _unpack(refs, n_in, n_out, n_cast):
    refs = list(refs)
    o0 = n_in + n_cast
    s0 = o0 + n_out + n_cast
    return refs[:n_in], refs[n_in:o0], refs[o0:o0 + n_out], refs[o0 + n_out:s0], refs[s0:]


def _call(body, *, grid, in_specs, args, out_specs, out_shapes, casts, scratch_shapes=(), name):
    outs = pl.pallas_call(
        functools.partial(body, n_cast=len(casts)),
        grid=(grid,),
        in_specs=list(in_specs) + [c.in_spec for c in casts],
        out_specs=list(out_specs) + [c.out_spec for c in casts],
        out_shape=list(out_shapes) + [c.out_shape for c in casts],
        scratch_shapes=list(scratch_shapes),
        compiler_params=_params(),
        name=name,
    )(*args, *[c.src for c in casts])
    n_out = len(out_specs)
    return outs[:n_out], outs[n_out:]


def _ffn_math(x, g_ref, w_in_ref, w_down_ref):
    xn = _rms(x, g_ref[0:1, :]).astype(_BF)
    acc = None
    for c in range(D_FF // FF_CHUNK):
        c0 = c * FF_CHUNK
        gate = _dot(xn, w_in_ref[:, c0:c0 + FF_CHUNK])
        up = _dot(xn, w_in_ref[:, D_FF + c0:D_FF + c0 + FF_CHUNK])
        act = (gate * jax.nn.sigmoid(gate) * up).astype(_BF)
        d = _dot(act, w_down_ref[c0:c0 + FF_CHUNK, :])
        acc = d if acc is None else acc + d
    return x + FFN_HALF * _rms(acc, g_ref[1:2, :])


def _ffn_body(*refs, split_in, split_out, n_cast):
    n_x = 2 if split_in else 1
    ins, cast_in, outs, cast_out, _ = _unpack(refs, n_x + 3, 2 if split_out else 1, n_cast)
    x_refs = ins[:n_x]
    g_ref, w_in_ref, w_down_ref = ins[n_x:]
    i = pl.program_id(0)
    if split_in:
        x = jnp.where(i < N_PROMPT_TILES, x_refs[0][...], x_refs[1][...])
    else:
        x = x_refs[0][...]
    y = _ffn_math(x, g_ref, w_in_ref, w_down_ref)
    if split_out:
        @pl.when(i < N_PROMPT_TILES)
        def _():
            outs[0][...] = y

        @pl.when(i == N_PROMPT_TILES)
        def _():
            outs[1][...] = y
    else:
        outs[0][...] = y
    _run_casts(cast_in, cast_out)


def _ffn(xs, g2, w_in, w_down, casts=(), *, split_in=False, split_out=False, name):
    split_specs = [pl.BlockSpec((TM, D), lambda i: (_prompt_tile_index(i), 0)),
                   pl.BlockSpec((TM, D), lambda i: (0, 0))]
    split_shapes = [jax.ShapeDtypeStruct((N_PROMPT_TILES * TM, D), _F32),
                    jax.ShapeDtypeStruct((TM, D), _F32)]
    return _call(
        functools.partial(_ffn_body, split_in=split_in, split_out=split_out),
        grid=N_TILES,
        in_specs=(split_specs if split_in else [_tile_spec()])
        + [_const_spec((2, D)), _const_spec((D, 2 * D_FF)), _const_spec((D_FF, D))],
        args=(*xs, g2, w_in, w_down),
        out_specs=split_specs if split_out else [_tile_spec()],
        out_shapes=split_shapes if split_out else [jax.ShapeDtypeStruct((ROWS, D), _F32)],
        casts=list(casts), name=name)


_HIST_BB = 32


def _hist_body(sta_ref, cw_ref, stp_ref, ha_ref, hp_ref):
    for l in range(DEC_SEQ):
        acc = jnp.zeros((_HIST_BB, D), _F32)
        for j in range(l, CONV_A - 1):
            acc = acc + sta_ref[j] * cw_ref[j - l:j - l + 1, :]
        ha_ref[l] = acc
        for gi, w in enumerate(POOL_WINDOWS):
            lanes = slice(gi * POOL_GD, (gi + 1) * POOL_GD)
            tot = jnp.zeros((_HIST_BB, POOL_GD), _F32)
            for j in range(l + 1, w):
                tot = tot + stp_ref[POOL_PAD + l - j, :, lanes]
            hp_ref[l, :, lanes] = tot


def _hist(st_a_t, cw, st_p_t):
    out = jax.ShapeDtypeStruct((DEC_SEQ, DEC_BATCH, D), _F32)
    ha, hp = pl.pallas_call(
        _hist_body,
        grid=(DEC_BATCH // _HIST_BB,),
        in_specs=[pl.BlockSpec((CONV_A - 1, _HIST_BB, D), lambda i: (0, i, 0)),
                  pl.BlockSpec((CONV_A, D), lambda i: (0, 0)),
                  pl.BlockSpec((POOL_PAD, _HIST_BB, D), lambda i: (0, i, 0))],
        out_specs=[pl.BlockSpec((DEC_SEQ, _HIST_BB, D), lambda i: (0, i, 0)),
                   pl.BlockSpec((DEC_SEQ, _HIST_BB, D), lambda i: (0, i, 0))],
        out_shape=[out, out],
        compiler_params=_params(),
        name="sample_hist",
    )(st_a_t, cw, st_p_t)
    return ha.reshape(TM, D), hp.reshape(TM, D)


_A_HIST = 32
_A_FIRST = _A_HIST - (CONV_A - 1)
_A_LB = 256
_A_RB = 64
_A_SHIFT_ROWS = 536


def _mix_a_body(*refs, n_cast):
    ins, cast_in, outs, cast_out, scratch = _unpack(refs, 9, 3, n_cast)
    h_ref, g_ref, w_in_ref, cw_ref, cb_ref, lng_ref, lnb_ref, w_out_ref, ha_ref = ins
    o_ref, sap_ref, glu_s_ref = outs
    ext_ref, cwb_ref, c_ref, sh_ref = scratch
    i = pl.program_id(0)

    @pl.when(i == 0)
    def _():
        for k in range(CONV_A):
            cwb_ref[k] = jnp.broadcast_to(cw_ref[k:k + 1, :], (SUBLANES, D))

    x = h_ref[...]
    xm = _rms(x, g_ref[0:1, :]).astype(_BF)
    a = _dot(xm, w_in_ref[...])
    glu = a[:, :D] * jax.nn.sigmoid(a[:, D:])

    def blocks(v, rows, lanes):
        return v.reshape(rows // SUBLANES, SUBLANES, lanes)

    ext_ref[_A_HIST:_A_HIST + TM, :] = glu

    @pl.when(i < N_PROMPT_TILES)
    def _():
        @pl.when(i % TILES_PER_SEQ == 0)
        def _():
            ext_ref[0:_A_HIST, :] = jnp.zeros((_A_HIST, D), _F32)

        for lb in range(D // _A_LB):
            lanes = slice(lb * _A_LB, (lb + 1) * _A_LB)
            for r in range(1, SUBLANES):
                sh_ref[r - 1] = ext_ref[r:r + _A_SHIFT_ROWS, lanes]
            bias = jnp.broadcast_to(cb_ref[:, lanes], (SUBLANES, _A_LB))[None]
            for rb in range(TM // _A_RB):
                r0 = rb * _A_RB
                acc = jnp.broadcast_to(bias, (_A_RB // SUBLANES, SUBLANES, _A_LB))
                for k in range(CONV_A):
                    r = (_A_FIRST + k) % SUBLANES
                    s = r0 + _A_FIRST + k - r
                    if r == 0:
                        rows = ext_ref[s:s + _A_RB, lanes]
                    else:
                        rows = sh_ref[r - 1, s:s + _A_RB, :]
                    acc = acc + blocks(rows, _A_RB, _A_LB) * cwb_ref[k][:, lanes][None]
                c_ref[r0:r0 + _A_RB, lanes] = acc.reshape(_A_RB, _A_LB)
        sap_ref[0] = ext_ref[TM + _A_FIRST:TM + _A_HIST, :]
        ext_ref[0:_A_HIST, :] = ext_ref[TM:TM + _A_HIST, :]

    @pl.when(i == N_PROMPT_TILES)
    def _():
        glu_s_ref[...] = glu
        ext_ref[0:TM, :] = glu
        bias = jnp.broadcast_to(cb_ref[...], (SUBLANES, D))[None]
        for l in range(DEC_SEQ):
            for rb in range(DEC_BATCH // _A_RB):
                r0 = l * DEC_BATCH + rb * _A_RB
                acc = blocks(ha_ref[r0:r0 + _A_RB, :], _A_RB, D) + bias
                for lp in range(l + 1):
                    s = lp * DEC_BATCH + rb * _A_RB
                    acc = acc + blocks(ext_ref[s:s + _A_RB, :], _A_RB, D) * cwb_ref[CONV_A - 1 + lp - l][None]
                c_ref[r0:r0 + _A_RB, :] = acc.reshape(_A_RB, D)

    c = _ln(c_ref[...], lng_ref[...], lnb_ref[...])
    c = c * jax.nn.sigmoid(c)
    m = _dot(c.astype(_BF), w_out_ref[...])
    o_ref[...] = x + _rms(m, g_ref[1:2, :])
    _run_casts(cast_in, cast_out)


def _mix_a(h, g2, w_in, cw, cb, lng, lnb, w_out, ha, casts):
    return _call(
        _mix_a_body,
        grid=N_TILES,
        in_specs=[_tile_spec(), _const_spec((2, D)), _const_spec((D, 2 * D)),
                  _const_spec((CONV_A, D)), _const_spec((1, D)), _const_spec((1, D)),
                  _const_spec((1, D)), _const_spec((D, D)), _const_spec((TM, D))],
        args=(h, g2, w_in, cw, cb, lng, lnb, w_out, ha),
        out_specs=[_tile_spec(),
                   pl.BlockSpec((1, CONV_A - 1, D),
                                lambda i: (_prompt_tile_index(i) // TILES_PER_SEQ, 0, 0)),
                   pl.BlockSpec((TM, D), lambda i: (0, 0))],
        out_shapes=[jax.ShapeDtypeStruct((ROWS, D), _F32),
                    jax.ShapeDtypeStruct((BATCH, CONV_A - 1, D), _F32),
                    jax.ShapeDtypeStruct((TM, D), _F32)],
        scratch_shapes=[pltpu.VMEM((_A_HIST + TM, D), _F32),
                        pltpu.VMEM((CONV_A, SUBLANES, D), _F32),
                        pltpu.VMEM((TM, D), _F32),
                        pltpu.VMEM((SUBLANES - 1, _A_SHIFT_ROWS, _A_LB), _F32)],
        casts=casts, name="mix_a")


_B_HIST = 8


def _mix_b_body(*refs, n_cast):
    ins, cast_in, outs, cast_out, scratch = _unpack(refs, 6, 3, n_cast)
    h_ref, g_ref, w_in_ref, cw_ref, w_out_ref, st_ref = ins
    o_ref, sbp_ref, z_s_ref = outs
    ext_ref, c_ref = scratch
    i = pl.program_id(0)
    x = h_ref[...]
    xm = _rms(x, g_ref[0:1, :]).astype(_BF)
    p = _dot(xm, w_in_ref[...])
    b_gate = p[:, :D]
    z = p[:, D:2 * D] * p[:, 2 * D:]
    w0 = cw_ref[0:1, :]
    w1 = cw_ref[1:2, :]
    w2 = cw_ref[2:3, :]

    @pl.when(i < N_PROMPT_TILES)
    def _():
        @pl.when(i % TILES_PER_SEQ == 0)
        def _():
            ext_ref[0:_B_HIST, :] = jnp.zeros((_B_HIST, D), _F32)

        ext_ref[_B_HIST:_B_HIST + TM, :] = z
        c_ref[...] = (ext_ref[_B_HIST - 2:_B_HIST - 2 + TM, :] * w0
                      + ext_ref[_B_HIST - 1:_B_HIST - 1 + TM, :] * w1
                      + z * w2)
        sbp_ref[0] = z[TM - (CONV_B - 1):, :]
        ext_ref[0:_B_HIST, :] = ext_ref[TM:TM + _B_HIST, :]

    @pl.when(i == N_PROMPT_TILES)
    def _():
        zs = [st_ref[0], st_ref[1]] + [z[l * DEC_BATCH:(l + 1) * DEC_BATCH, :] for l in range(DEC_SEQ)]
        for l in range(DEC_SEQ):
            c_ref[l * DEC_BATCH:(l + 1) * DEC_BATCH, :] = zs[l] * w0 + zs[l + 1] * w1 + zs[l + 2] * w2
        z_s_ref[...] = z[(DEC_SEQ - (CONV_B - 1)) * DEC_BATCH:, :]

    m = _dot((b_gate * c_ref[...]).astype(_BF), w_out_ref[...])
    o_ref[...] = x + _rms(m, g_ref[1:2, :])
    _run_casts(cast_in, cast_out)


def _mix_b(h, g2, w_in, cw, w_out, st_t, casts):
    n_keep = CONV_B - 1
    return _call(
        _mix_b_body,
        grid=N_TILES,
        in_specs=[_tile_spec(), _const_spec((2, D)), _const_spec((D, 3 * D)),
                  _const_spec((CONV_B, D)), _const_spec((D, D)),
                  _const_spec((n_keep, DEC_BATCH, D))],
        args=(h, g2, w_in, cw, w_out, st_t),
        out_specs=[_tile_spec(),
                   pl.BlockSpec((1, n_keep, D),
                                lambda i: (_prompt_tile_index(i) // TILES_PER_SEQ, 0, 0)),
                   pl.BlockSpec((n_keep * DEC_BATCH, D), lambda i: (0, 0))],
        out_shapes=[jax.ShapeDtypeStruct((ROWS, D), _F32),
                    jax.ShapeDtypeStruct((BATCH, n_keep, D), _F32),
                    jax.ShapeDtypeStruct((n_keep * DEC_BATCH, D), _F32)],
        scratch_shapes=[pltpu.VMEM((_B_HIST + TM, D), _F32),
                        pltpu.VMEM((TM, D), _F32)],
        casts=casts, name="mix_b")


_HD = D // HEADS


def _mix_c_body(*refs, n_cast):
    ins, cast_in, outs, cast_out, scratch = _unpack(refs, 10, 2, n_cast)
    h_ref, g_ref, w_in_ref, lng_ref, lnb_ref, ws_ref, bst_ref, wsv_ref, bsv_ref, w_out_ref = ins
    o_ref, v_s_ref = outs
    (s_ref,) = scratch
    i = pl.program_id(0)
    x = h_ref[...]
    xm = _rms(x, g_ref[0:1, :]).astype(_BF)
    zz = _gelu(_dot(xm, w_in_ref[...]))
    u = zz[:, :D]
    v = _ln(zz[:, D:], lng_ref[...], lnb_ref[...])

    @pl.when(i < N_PROMPT_TILES)
    def _():
        vb = v.astype(_BF)
        n_chunks = TM // CHUNK
        row = lax.broadcasted_iota(jnp.int32, (CHUNK, CHUNK), 0)
        col = lax.broadcasted_iota(jnp.int32, (CHUNK, CHUNK), 1)
        causal = col <= row
        for hd in range(HEADS):
            lanes = slice(hd * _HD, (hd + 1) * _HD)
            wm = jnp.where(causal, ws_ref[hd], 0.0).astype(_BF)
            rhs = jnp.concatenate([vb[n * CHUNK:(n + 1) * CHUNK, lanes] for n in range(n_chunks)], axis=1)
            out = _dot(wm, rhs) + bst_ref[:, hd:hd + 1]
            for n in range(n_chunks):
                s_ref[n * CHUNK:(n + 1) * CHUNK, lanes] = out[:, n * CHUNK:(n + 1) * CHUNK]

    @pl.when(i == N_PROMPT_TILES)
    def _():
        v_s_ref[...] = v
        for l in range(DEC_SEQ):
            acc = jnp.broadcast_to(bsv_ref[l:l + 1, :], (DEC_BATCH, D))
            for lp in range(l + 1):
                acc = acc + v[lp * DEC_BATCH:(lp + 1) * DEC_BATCH, :] * wsv_ref[l, lp:lp + 1, :]
            s_ref[l * DEC_BATCH:(l + 1) * DEC_BATCH, :] = acc

    m = _dot((u * s_ref[...]).astype(_BF), w_out_ref[...])
    o_ref[...] = x + _rms(m, g_ref[1:2, :])
    _run_casts(cast_in, cast_out)


def _mix_c(h, g2, w_in, lng, lnb, ws, bst, wsv, bsv, w_out, casts):
    return _call(
        _mix_c_body,
        grid=N_TILES,
        in_specs=[_tile_spec(), _const_spec((2, D)), _const_spec((D, 2 * D)),
                  _const_spec((1, D)), _const_spec((1, D)),
                  _const_spec((HEADS, CHUNK, CHUNK)), _const_spec((CHUNK, HEADS)),
                  _const_spec((DEC_SEQ, DEC_SEQ, D)), _const_spec((DEC_SEQ, D)),
                  _const_spec((D, D))],
        args=(h, g2, w_in, lng, lnb, ws, bst, wsv, bsv, w_out),
        out_specs=[_tile_spec(), pl.BlockSpec((TM, D), lambda i: (0, 0))],
        out_shapes=[jax.ShapeDtypeStruct((ROWS, D), _F32),
                    jax.ShapeDtypeStruct((TM, D), _F32)],
        scratch_shapes=[pltpu.VMEM((TM, D), _F32)],
        casts=casts, name="mix_c")


_D_ZERO = 8
_D_HIST = 24
_D_ROWS = _D_HIST + TM


def _mix_d_body(*refs, n_cast):
    ins, cast_in, outs, cast_out, scratch = _unpack(refs, 5, 3, n_cast)
    h_ref, g_ref, wg_ref, scale_ref, hp_ref = ins
    o_ref, spp_ref, xm_s_ref = outs
    ext_ref, sa_ref, sb_ref, y_ref = scratch
    i = pl.program_id(0)
    x = h_ref[...]
    xm = _rms(x, g_ref[0:1, :])
    g1, g2, g3 = POOL_GD, 2 * POOL_GD, 3 * POOL_GD

    @pl.when(i == 0)
    def _():
        for ref in (ext_ref, sa_ref, sb_ref):
            ref[0:_D_ZERO, :] = jnp.zeros((_D_ZERO, D), _F32)

    @pl.when(i < N_PROMPT_TILES)
    def _():
        @pl.when(i % TILES_PER_SEQ == 0)
        def _():
            ext_ref[_D_ZERO:_D_HIST, :] = jnp.zeros((_D_HIST - _D_ZERO, D), _F32)

        ext_ref[_D_HIST:_D_ROWS, :] = xm
        lo, hi = _D_ZERO, _D_ROWS
        sa_ref[lo:hi, :] = ext_ref[lo:hi, :] + ext_ref[lo - 1:hi - 1, :]
        sb_ref[lo:hi, g1:] = sa_ref[lo:hi, g1:] + sa_ref[lo - 2:hi - 2, g1:]
        sa_ref[lo:hi, g2:] = sb_ref[lo:hi, g2:] + sb_ref[lo - 4:hi - 4, g2:]
        tots = (sa_ref[_D_HIST:hi, 0:g1], sb_ref[_D_HIST:hi, g1:g2], sa_ref[_D_HIST:hi, g2:g3],
                sa_ref[_D_HIST:hi, g3:] + sa_ref[_D_HIST - 8:hi - 8, g3:])
        pos = (i % TILES_PER_SEQ) * TM + lax.broadcasted_iota(jnp.int32, (TM, 1), 0)
        for gi, w in enumerate(POOL_WINDOWS):
            lanes = slice(gi * POOL_GD, (gi + 1) * POOL_GD)
            cnt = jnp.minimum(w, pos + 1).astype(_F32)
            diff = tots[gi] / cnt - xm[:, lanes]
            y_ref[:, lanes] = _dot(diff.astype(_BF), wg_ref[gi * POOL_GD:(gi + 1) * POOL_GD, :])
        spp_ref[0] = ext_ref[_D_ROWS - POOL_PAD:_D_ROWS, :]
        ext_ref[_D_ZERO:_D_HIST, :] = ext_ref[TM + _D_ZERO:_D_ROWS, :]

    @pl.when(i == N_PROMPT_TILES)
    def _():
        xm_s_ref[...] = xm
        for gi, w in enumerate(POOL_WINDOWS):
            lanes = slice(gi * POOL_GD, (gi + 1) * POOL_GD)
            for l in range(DEC_SEQ):
                rows = slice(l * DEC_BATCH, (l + 1) * DEC_BATCH)
                cur = xm[rows, lanes]
                tot = cur + hp_ref[rows, lanes]
                for j in range(1, min(w - 1, l) + 1):
                    tot = tot + xm[(l - j) * DEC_BATCH:(l - j + 1) * DEC_BATCH, lanes]
                diff = tot / float(w) - cur
                y_ref[rows, lanes] = _dot(diff.astype(_BF), wg_ref[gi * POOL_GD:(gi + 1) * POOL_GD, :])

    m = y_ref[...] * scale_ref[...]
    o_ref[...] = x + _rms(m, g_ref[1:2, :])
    _run_casts(cast_in, cast_out)


def _mix_d(h, g2, wg, scale, hp, casts):
    return _call(
        _mix_d_body,
        grid=N_TILES,
        in_specs=[_tile_spec(), _const_spec((2, D)),
                  _const_spec((len(POOL_WINDOWS) * POOL_GD, POOL_GD)), _const_spec((1, D)),
                  _const_spec((TM, D))],
        args=(h, g2, wg, scale, hp),
        out_specs=[_tile_spec(),
                   pl.BlockSpec((1, POOL_PAD, D),
                                lambda i: (_prompt_tile_index(i) // TILES_PER_SEQ, 0, 0)),
                   pl.BlockSpec((TM, D), lambda i: (0, 0))],
        out_shapes=[jax.ShapeDtypeStruct((ROWS, D), _F32),
                    jax.ShapeDtypeStruct((BATCH, POOL_PAD, D), _F32),
                    jax.ShapeDtypeStruct((TM, D), _F32)],
        scratch_shapes=[pltpu.VMEM((_D_ROWS, D), _F32), pltpu.VMEM((_D_ROWS, D), _F32),
                        pltpu.VMEM((_D_ROWS, D), _F32), pltpu.VMEM((TM, D), _F32)],
        casts=casts, name="mix_d")


def _to_pos_major(a):
    return jnp.transpose(a, (1, 0, 2))


def _from_pos_major(a, length):
    return jnp.transpose(a.reshape(length, DEC_BATCH, D), (1, 0, 2))


def _advance_state(state, new_rows):
    length = state.shape[1]
    keep = length - DEC_SEQ
    zero = jnp.zeros((), state.dtype)
    kept = lax.pad(state[:, DEC_SEQ:, :], zero, ((0, 0, 0), (0, DEC_SEQ, 0), (0, 0, 0)))
    new = lax.pad(new_rows, zero, ((0, 0, 0), (keep, 0, 0), (0, 0, 0)))
    row = lax.broadcasted_iota(jnp.int32, (1, length, 1), 1)
    return jnp.where(row < keep, kept, new)


def kernel(x_prompt, x_sample, state_conv_a, state_conv_b, state_pool, norm_g, ffn_w_in, ffn_w_down,
           a_w_in, a_conv_w, a_conv_b, a_ln_g, a_ln_b, a_w_out, b_w_in, b_conv_w, b_w_out,
           c_w_in, c_ln_g, c_ln_b, c_ws, c_bs, c_w_out, d_w_group, d_scale):
    row = lambda a: a.reshape(1, D)
    ffn_casts = lambda layer, half: _ffn_weight_casts(ffn_w_in, ffn_w_down, layer, half)
    mats = lambda *ws: [_matrix_cast(w) for w in ws]

    xp = x_prompt.reshape(BATCH * SEQ, D)
    xs = _to_pos_major(x_sample).reshape(TM, D)
    ha, hp = _hist(_to_pos_major(state_conv_a), a_conv_w, _to_pos_major(state_pool))

    (h,), (w_in, w_down, a_in, a_out) = _ffn(
        (xp, xs), norm_g[0, 0:2], ffn_w_in[0, 0].astype(_BF), ffn_w_down[0, 0].astype(_BF),
        ffn_casts(0, 1) + mats(a_w_in, a_w_out), split_in=True, name="ffn_0_0")
    (h, sa_p, glu_s), _ = _mix_a(
        h, norm_g[0, 2:4], a_in, a_conv_w, row(a_conv_b), row(a_ln_g), row(a_ln_b), a_out, ha, [])
    (h,), (w_in, w_down) = _ffn((h,), norm_g[0, 4:6], w_in, w_down, ffn_casts(1, 0), name="ffn_0_1")

    (h,), (w_in, w_down, b_in, b_out) = _ffn(
        (h,), norm_g[1, 0:2], w_in, w_down, ffn_casts(1, 1) + mats(b_w_in, b_w_out), name="ffn_1_0")
    (h, sb_p, z_s), _ = _mix_b(h, norm_g[1, 2:4], b_in, b_conv_w, b_out, _to_pos_major(state_conv_b), [])
    (h,), (w_in, w_down) = _ffn((h,), norm_g[1, 4:6], w_in, w_down, ffn_casts(2, 0), name="ffn_1_1")

    (h,), (w_in, w_down, c_in, c_out) = _ffn(
        (h,), norm_g[2, 0:2], w_in, w_down, ffn_casts(2, 1) + mats(c_w_in, c_w_out), name="ffn_2_0")
    wsv = jnp.repeat(jnp.transpose(c_ws[:, :DEC_SEQ, :DEC_SEQ], (1, 2, 0)), _HD, axis=-1)
    bsv = jnp.repeat(jnp.transpose(c_bs[:, :DEC_SEQ]), _HD, axis=-1)
    (h, v_s), _ = _mix_c(h, norm_g[2, 2:4], c_in, row(c_ln_g), row(c_ln_b), c_ws, jnp.transpose(c_bs),
                         wsv, bsv, c_out, [])
    (h,), (w_in, w_down) = _ffn((h,), norm_g[2, 4:6], w_in, w_down, ffn_casts(3, 0), name="ffn_2_1")

    (h,), (w_in, w_down, d_wg) = _ffn(
        (h,), norm_g[3, 0:2], w_in, w_down,
        ffn_casts(3, 1) + mats(d_w_group.reshape(len(POOL_WINDOWS) * POOL_GD, POOL_GD)), name="ffn_3_0")
    (h, sp_p, xm_s), _ = _mix_d(h, norm_g[3, 2:4], d_wg, row(d_scale), hp, [])
    (yp, ys), _ = _ffn((h,), norm_g[3, 4:6], w_in, w_down, split_out=True, name="ffn_3_1")

    y_prompt = yp.reshape(BATCH, SEQ, D)
    y_sample = _from_pos_major(ys, DEC_SEQ)
    sa_s = _advance_state(state_conv_a, _from_pos_major(glu_s, DEC_SEQ))
    sb_s = _from_pos_major(z_s, CONV_B - 1)
    sc_s = _from_pos_major(v_s, DEC_SEQ)
    sp_s = _advance_state(state_pool, _from_pos_major(xm_s, DEC_SEQ))
    return (y_prompt, y_sample, sa_p, sa_s, sb_p, sb_s, sc_s, sp_p, sp_s)
```

```python
import functools
from typing import Any, NamedTuple

import jax
import jax.numpy as jnp
from jax import lax
from jax.experimental import pallas as pl
from jax.experimental.pallas import tpu as pltpu

D = 1024
D_FF = 2816
DEPTH = 4
EPS = 1e-6
FFN_HALF = 0.5
CONV_A = 31
CONV_B = 3
CHUNK = 128
HEADS = 8
POOL_WINDOWS = (2, 4, 8, 16)
POOL_GD = D // len(POOL_WINDOWS)
POOL_PAD = 15

BATCH = 8
SEQ = 2048
DEC_BATCH = 128
DEC_SEQ = 4

TM = 512
TILES_PER_SEQ = SEQ // TM
N_PROMPT_TILES = BATCH * SEQ // TM
N_TILES = N_PROMPT_TILES + 1
ROWS = N_TILES * TM
FF_CHUNK = 256
SUBLANES = 8
LANES = 128

FFN_CAST_STEPS = 22
MIX_CAST_STEPS = 8

_BF = jnp.bfloat16
_F32 = jnp.float32
_VMEM_LIMIT = 60 * 1024 * 1024


def _dot(a, b):
    return jnp.dot(a, b, preferred_element_type=_F32)


def _rms(x, g):
    return x * lax.rsqrt(jnp.mean(x * x, axis=-1, keepdims=True) + EPS) * g


def _ln(x, g, b):
    xc = x - jnp.mean(x, axis=-1, keepdims=True)
    return xc * lax.rsqrt(jnp.mean(xc * xc, axis=-1, keepdims=True) + EPS) * g + b


def _gelu(x):
    return 0.5 * x * (1.0 + lax.erf(x * (2.0 ** -0.5)))


def _const_spec(shape):
    zeros = (0,) * len(shape)
    return pl.BlockSpec(shape, lambda i: zeros, pipeline_mode=pl.Buffered(1))


def _tile_spec():
    return pl.BlockSpec((TM, D), lambda i: (i, 0))


def _prompt_tile_index(i):
    return jnp.minimum(i, N_PROMPT_TILES - 1)


def _params():
    return pltpu.CompilerParams(dimension_semantics=("arbitrary",),
                                vmem_limit_bytes=_VMEM_LIMIT)


class _Cast(NamedTuple):
    src: Any
    in_spec: Any
    out_spec: Any
    out_shape: Any


def _ffn_weight_casts(ffn_w_in, ffn_w_down, layer, half):
    n = FFN_CAST_STEPS
    cb = 2 * D_FF // n
    rb = D_FF // n
    step = lambda i: jnp.minimum(i, n - 1)
    return [
        _Cast(ffn_w_in,
              pl.BlockSpec((None, None, D, cb), lambda i: (layer, half, 0, step(i))),
              pl.BlockSpec((D, cb), lambda i: (0, step(i))),
              jax.ShapeDtypeStruct((D, 2 * D_FF), _BF)),
        _Cast(ffn_w_down,
              pl.BlockSpec((None, None, rb, D), lambda i: (layer, half, step(i), 0)),
              pl.BlockSpec((rb, D), lambda i: (step(i), 0)),
              jax.ShapeDtypeStruct((D_FF, D), _BF)),
    ]


def _matrix_cast(w):
    n = MIX_CAST_STEPS
    rows, cols = w.shape
    rb = rows // n
    step = lambda i: jnp.minimum(i, n - 1)
    return _Cast(w,
                 pl.BlockSpec((rb, cols), lambda i: (step(i), 0)),
                 pl.BlockSpec((rb, cols), lambda i: (step(i), 0)),
                 jax.ShapeDtypeStruct((rows, cols), _BF))


def _run_casts(in_refs, out_refs):
    for src_ref, dst_ref in zip(in_refs, out_refs):
        dst_ref[...] = src_ref[...].astype(_BF)


def _unpack(refs, n_in, n_out, n_cast):
    refs = list(refs)
    o0 = n_in + n_cast
    s0 = o0 + n_out + n_cast
    return refs[:n_in], refs[n_in:o0], refs[o0:o0 + n_out], refs[o0 + n_out:s0], refs[s0:]


def _call(body, *, grid, in_specs, args, out_specs, out_shapes, casts, scratch_shapes=(), name):
    outs = pl.pallas_call(
        functools.partial(body, n_cast=len(casts)),
        grid=(grid,),
        in_specs=list(in_specs) + [c.in_spec for c in casts],
        out_specs=list(out_specs) + [c.out_spec for c in casts],
        out_shape=list(out_shapes) + [c.out_shape for c in casts],
        scratch_shapes=list(scratch_shapes),
        compiler_params=_params(),
        name=name,
    )(*args, *[c.src for c in casts])
    n_out = len(out_specs)
    return outs[:n_out], outs[n_out:]


def _ffn_math(x, g_ref, w_in_ref, w_down_ref):
    xn = _rms(x, g_ref[0:1, :]).astype(_BF)
    acc = None
    for c in range(D_FF // FF_CHUNK):
        c0 = c * FF_CHUNK
        gate = _dot(xn, w_in_ref[:, c0:c0 + FF_CHUNK])
        up = _dot(xn, w_in_ref[:, D_FF + c0:D_FF + c0 + FF_CHUNK])
        act = (gate * jax.nn.sigmoid(gate) * up).astype(_BF)
        d = _dot(act, w_down_ref[c0:c0 + FF_CHUNK, :])
        acc = d if acc is None else acc + d
    return x + FFN_HALF * _rms(acc, g_ref[1:2, :])


def _ffn_body(*refs, split_in, split_out, n_cast):
    n_x = 2 if split_in else 1
    ins, cast_in, outs, cast_out, _ = _unpack(refs, n_x + 3, 2 if split_out else 1, n_cast)
    x_refs = ins[:n_x]
    g_ref, w_in_ref, w_down_ref = ins[n_x:]
    i = pl.program_id(0)
    if split_in:
        x = jnp.where(i < N_PROMPT_TILES, x_refs[0][...], x_refs[1][...])
    else:
        x = x_refs[0][...]
    y = _ffn_math(x, g_ref, w_in_ref, w_down_ref)
    if split_out:
        @pl.when(i < N_PROMPT_TILES)
        def _():
            outs[0][...] = y

        @pl.when(i == N_PROMPT_TILES)
        def _():
            outs[1][...] = y
    else:
        outs[0][...] = y
    _run_casts(cast_in, cast_out)


def _ffn(xs, g2, w_in, w_down, casts=(), *, split_in=False, split_out=False, name):
    split_specs = [pl.BlockSpec((TM, D), lambda i: (_prompt_tile_index(i), 0)),
                   pl.BlockSpec((TM, D), lambda i: (0, 0))]
    split_shapes = [jax.ShapeDtypeStruct((N_PROMPT_TILES * TM, D), _F32),
                    jax.ShapeDtypeStruct((TM, D), _F32)]
    return _call(
        functools.partial(_ffn_body, split_in=split_in, split_out=split_out),
        grid=N_TILES,
        in_specs=(split_specs if split_in else [_tile_spec()])
        + [_const_spec((2, D)), _const_spec((D, 2 * D_FF)), _const_spec((D_FF, D))],
        args=(*xs, g2, w_in, w_down),
        out_specs=split_specs if split_out else [_tile_spec()],
        out_shapes=split_shapes if split_out else [jax.ShapeDtypeStruct((ROWS, D), _F32)],
        casts=list(casts), name=name)


_HIST_BB = 32


def _hist_body(sta_ref, cw_ref, stp_ref, ha_ref, hp_ref):
    for l in range(DEC_SEQ):
        acc = jnp.zeros((_HIST_BB, D), _F32)
        for j in range(l, CONV_A - 1):
            acc = acc + sta_ref[j] * cw_ref[j - l:j - l + 1, :]
        ha_ref[l] = acc
        for gi, w in enumerate(POOL_WINDOWS):
            lanes = slice(gi * POOL_GD, (gi + 1) * POOL_GD)
            tot = jnp.zeros((_HIST_BB, POOL_GD), _F32)
            for j in range(l + 1, w):
                tot = tot + stp_ref[POOL_PAD + l - j, :, lanes]
            hp_ref[l, :, lanes] = tot


def _hist(st_a_t, cw, st_p_t):
    out = jax.ShapeDtypeStruct((DEC_SEQ, DEC_BATCH, D), _F32)
    ha, hp = pl.pallas_call(
        _hist_body,
        grid=(DEC_BATCH // _HIST_BB,),
        in_specs=[pl.BlockSpec((CONV_A - 1, _HIST_BB, D), lambda i: (0, i, 0)),
                  pl.BlockSpec((CONV_A, D), lambda i: (0, 0)),
                  pl.BlockSpec((POOL_PAD, _HIST_BB, D), lambda i: (0, i, 0))],
        out_specs=[pl.BlockSpec((DEC_SEQ, _HIST_BB, D), lambda i: (0, i, 0)),
                   pl.BlockSpec((DEC_SEQ, _HIST_BB, D), lambda i: (0, i, 0))],
        out_shape=[out, out],
        compiler_params=_params(),
        name="sample_hist",
    )(st_a_t, cw, st_p_t)
    return ha.reshape(TM, D), hp.reshape(TM, D)


_A_HIST = 32
_A_FIRST = _A_HIST - (CONV_A - 1)
_A_LB = 256
_A_RB = 64
_A_SHIFT_ROWS = 536


def _mix_a_body(*refs, n_cast):
    ins, cast_in, outs, cast_out, scratch = _unpack(refs, 9, 3, n_cast)
    h_ref, g_ref, w_in_ref, cw_ref, cb_ref, lng_ref, lnb_ref, w_out_ref, ha_ref = ins
    o_ref, sap_ref, glu_s_ref = outs
    ext_ref, cwb_ref, c_ref, sh_ref = scratch
    i = pl.program_id(0)

    @pl.when(i == 0)
    def _():
        for k in range(CONV_A):
            cwb_ref[k] = jnp.broadcast_to(cw_ref[k:k + 1, :], (SUBLANES, D))

    x = h_ref[...]
    xm = _rms(x, g_ref[0:1, :]).astype(_BF)
    a = _dot(xm, w_in_ref[...])
    glu = a[:, :D] * jax.nn.sigmoid(a[:, D:])

    def blocks(v, rows, lanes):
        return v.reshape(rows // SUBLANES, SUBLANES, lanes)

    ext_ref[_A_HIST:_A_HIST + TM, :] = glu

    @pl.when(i < N_PROMPT_TILES)
    def _():
        @pl.when(i % TILES_PER_SEQ == 0)
        def _():
            ext_ref[0:_A_HIST, :] = jnp.zeros((_A_HIST, D), _F32)

        for lb in range(D // _A_LB):
            lanes = slice(lb * _A_LB, (lb + 1) * _A_LB)
            for r in range(1, SUBLANES):
                sh_ref[r - 1] = ext_ref[r:r + _A_SHIFT_ROWS, lanes]
            bias = jnp.broadcast_to(cb_ref[:, lanes], (SUBLANES, _A_LB))[None]
            for rb in range(TM // _A_RB):
                r0 = rb * _A_RB
                acc = jnp.broadcast_to(bias, (_A_RB // SUBLANES, SUBLANES, _A_LB))
                for k in range(CONV_A):
                    r = (_A_FIRST + k) % SUBLANES
                    s = r0 + _A_FIRST + k - r
                    if r == 0:
                        rows = ext_ref[s:s + _A_RB, lanes]
                    else:
                        rows = sh_ref[r - 1, s:s + _A_RB, :]
                    acc = acc + blocks(rows, _A_RB, _A_LB) * cwb_ref[k][:, lanes][None]
                c_ref[r0:r0 + _A_RB, lanes] = acc.reshape(_A_RB, _A_LB)
        sap_ref[0] = ext_ref[TM + _A_FIRST:TM + _A_HIST, :]
        ext_ref[0:_A_HIST, :] = ext_ref[TM:TM + _A_HIST, :]

    @pl.when(i == N_PROMPT_TILES)
    def _():
        glu_s_ref[...] = glu
        ext_ref[0:TM, :] = glu
        bias = jnp.broadcast_to(cb_ref[...], (SUBLANES, D))[None]
        for l in range(DEC_SEQ):
            for rb in range(DEC_BATCH // _A_RB):
                r0 = l * DEC_BATCH + rb * _A_RB
                acc = blocks(ha_ref[r0:r0 + _A_RB, :], _A_RB, D) + bias
                for lp in range(l + 1):
                    s = lp * DEC_BATCH + rb * _A_RB
                    acc = acc + blocks(ext_ref[s:s + _A_RB, :], _A_RB, D) * cwb_ref[CONV_A - 1 + lp - l][None]
                c_ref[r0:r0 + _A_RB, :] = acc.reshape(_A_RB, D)

    c = _ln(c_ref[...], lng_ref[...], lnb_ref[...])
    c = c * jax.nn.sigmoid(c)
    m = _dot(c.astype(_BF), w_out_ref[...])
    o_ref[...] = x + _rms(m, g_ref[1:2, :])
    _run_casts(cast_in, cast_out)


def _mix_a(h, g2, w_in, cw, cb, lng, lnb, w_out, ha, casts):
    return _call(
        _mix_a_body,
        grid=N_TILES,
        in_specs=[_tile_spec(), _const_spec((2, D)), _const_spec((D, 2 * D)),
                  _const_spec((CONV_A, D)), _const_spec((1, D)), _const_spec((1, D)),
                  _const_spec((1, D)), _const_spec((D, D)), _const_spec((TM, D))],
        args=(h, g2, w_in, cw, cb, lng, lnb, w_out, ha),
        out_specs=[_tile_spec(),
                   pl.BlockSpec((1, CONV_A - 1, D),
                                lambda i: (_prompt_tile_index(i) // TILES_PER_SEQ, 0, 0)),
                   pl.BlockSpec((TM, D), lambda i: (0, 0))],
        out_shapes=[jax.ShapeDtypeStruct((ROWS, D), _F32),
                    jax.ShapeDtypeStruct((BATCH, CONV_A - 1, D), _F32),
                    jax.ShapeDtypeStruct((TM, D), _F32)],
        scratch_shapes=[pltpu.VMEM((_A_HIST + TM, D), _F32),
                        pltpu.VMEM((CONV_A, SUBLANES, D), _F32),
                        pltpu.VMEM((TM, D), _F32),
                        pltpu.VMEM((SUBLANES - 1, _A_SHIFT_ROWS, _A_LB), _F32)],
        casts=casts, name="mix_a")


_B_HIST = 8


def _mix_b_body(*refs, n_cast):
    ins, cast_in, outs, cast_out, scratch = _unpack(refs, 6, 3, n_cast)
    h_ref, g_ref, w_in_ref, cw_ref, w_out_ref, st_ref = ins
    o_ref, sbp_ref, z_s_ref = outs
    ext_ref, c_ref = scratch
    i = pl.program_id(0)
    x = h_ref[...]
    xm = _rms(x, g_ref[0:1, :]).astype(_BF)
    p = _dot(xm, w_in_ref[...])
    b_gate = p[:, :D]
    z = p[:, D:2 * D] * p[:, 2 * D:]
    w0 = cw_ref[0:1, :]
    w1 = cw_ref[1:2, :]
    w2 = cw_ref[2:3, :]

    ext_ref[_B_HIST:_B_HIST + TM, :] = z

    @pl.when(i < N_PROMPT_TILES)
    def _():
        @pl.when(i % TILES_PER_SEQ == 0)
        def _():
            ext_ref[0:_B_HIST, :] = jnp.zeros((_B_HIST, D), _F32)

        c_ref[...] = (ext_ref[_B_HIST - 2:_B_HIST - 2 + TM, :] * w0
                      + ext_ref[_B_HIST - 1:_B_HIST - 1 + TM, :] * w1
                      + z * w2)
        sbp_ref[0] = z[TM - (CONV_B - 1):, :]
        ext_ref[0:_B_HIST, :] = ext_ref[TM:TM + _B_HIST, :]

    @pl.when(i == N_PROMPT_TILES)
    def _():
        zs = [st_ref[0], st_ref[1]] + [z[l * DEC_BATCH:(l + 1) * DEC_BATCH, :] for l in range(DEC_SEQ)]
        for l in range(DEC_SEQ):
            c_ref[l * DEC_BATCH:(l + 1) * DEC_BATCH, :] = zs[l] * w0 + zs[l + 1] * w1 + zs[l + 2] * w2
        z_s_ref[...] = z[(DEC_SEQ - (CONV_B - 1)) * DEC_BATCH:, :]

    m = _dot((b_gate * c_ref[...]).astype(_BF), w_out_ref[...])
    o_ref[...] = x + _rms(m, g_ref[1:2, :])
    _run_casts(cast_in, cast_out)


def _mix_b(h, g2, w_in, cw, w_out, st_t, casts):
    n_keep = CONV_B - 1
    return _call(
        _mix_b_body,
        grid=N_TILES,
        in_specs=[_tile_spec(), _const_spec((2, D)), _const_spec((D, 3 * D)),
                  _const_spec((CONV_B, D)), _const_spec((D, D)),
                  _const_spec((n_keep, DEC_BATCH, D))],
        args=(h, g2, w_in, cw, w_out, st_t),
        out_specs=[_tile_spec(),
                   pl.BlockSpec((1, n_keep, D),
                                lambda i: (_prompt_tile_index(i) // TILES_PER_SEQ, 0, 0)),
                   pl.BlockSpec((n_keep * DEC_BATCH, D), lambda i: (0, 0))],
        out_shapes=[jax.ShapeDtypeStruct((ROWS, D), _F32),
                    jax.ShapeDtypeStruct((BATCH, n_keep, D), _F32),
                    jax.ShapeDtypeStruct((n_keep * DEC_BATCH, D), _F32)],
        scratch_shapes=[pltpu.VMEM((_B_HIST + TM, D), _F32),
                        pltpu.VMEM((TM, D), _F32)],
        casts=casts, name="mix_b")


_HD = D // HEADS


def _mix_c_body(*refs, n_cast):
    ins, cast_in, outs, cast_out, scratch = _unpack(refs, 10, 2, n_cast)
    h_ref, g_ref, w_in_ref, lng_ref, lnb_ref, ws_ref, bst_ref, wsv_ref, bsv_ref, w_out_ref = ins
    o_ref, v_s_ref = outs
    (s_ref,) = scratch
    i = pl.program_id(0)
    x = h_ref[...]
    xm = _rms(x, g_ref[0:1, :]).astype(_BF)
    zz = _gelu(_dot(xm, w_in_ref[...]))
    u = zz[:, :D]
    v = _ln(zz[:, D:], lng_ref[...], lnb_ref[...])

    @pl.when(i < N_PROMPT_TILES)
    def _():
        vb = v.astype(_BF)
        n_chunks = TM // CHUNK
        row = lax.broadcasted_iota(jnp.int32, (CHUNK, CHUNK), 0)
        col = lax.broadcasted_iota(jnp.int32, (CHUNK, CHUNK), 1)
        causal = col <= row
        for hd in range(HEADS):
            lanes = slice(hd * _HD, (hd + 1) * _HD)
            wm = jnp.where(causal, ws_ref[hd], 0.0).astype(_BF)
            rhs = jnp.concatenate([vb[n * CHUNK:(n + 1) * CHUNK, lanes] for n in range(n_chunks)], axis=1)
            out = _dot(wm, rhs) + bst_ref[:, hd:hd + 1]
            for n in range(n_chunks):
                s_ref[n * CHUNK:(n + 1) * CHUNK, lanes] = out[:, n * CHUNK:(n + 1) * CHUNK]

    @pl.when(i == N_PROMPT_TILES)
    def _():
        v_s_ref[...] = v
        for l in range(DEC_SEQ):
            acc = jnp.broadcast_to(bsv_ref[l:l + 1, :], (DEC_BATCH, D))
            for lp in range(l + 1):
                acc = acc + v[lp * DEC_BATCH:(lp + 1) * DEC_BATCH, :] * wsv_ref[l, lp:lp + 1, :]
            s_ref[l * DEC_BATCH:(l + 1) * DEC_BATCH, :] = acc

    m = _dot((u * s_ref[...]).astype(_BF), w_out_ref[...])
    o_ref[...] = x + _rms(m, g_ref[1:2, :])
    _run_casts(cast_in, cast_out)


def _mix_c(h, g2, w_in, lng, lnb, ws, bst, wsv, bsv, w_out, casts):
    return _call(
        _mix_c_body,
        grid=N_TILES,
        in_specs=[_tile_spec(), _const_spec((2, D)), _const_spec((D, 2 * D)),
                  _const_spec((1, D)), _const_spec((1, D)),
                  _const_spec((HEADS, CHUNK, CHUNK)), _const_spec((CHUNK, HEADS)),
                  _const_spec((DEC_SEQ, DEC_SEQ, D)), _const_spec((DEC_SEQ, D)),
                  _const_spec((D, D))],
        args=(h, g2, w_in, lng, lnb, ws, bst, wsv, bsv, w_out),
        out_specs=[_tile_spec(), pl.BlockSpec((TM, D), lambda i: (0, 0))],
        out_shapes=[jax.ShapeDtypeStruct((ROWS, D), _F32),
                    jax.ShapeDtypeStruct((TM, D), _F32)],
        scratch_shapes=[pltpu.VMEM((TM, D), _F32)],
        casts=casts, name="mix_c")


_D_ZERO = 8
_D_HIST = 24
_D_ROWS = _D_HIST + TM


def _mix_d_body(*refs, n_cast):
    ins, cast_in, outs, cast_out, scratch = _unpack(refs, 5, 3, n_cast)
    h_ref, g_ref, wg_ref, scale_ref, hp_ref = ins
    o_ref, spp_ref, xm_s_ref = outs
    ext_ref, sa_ref, sb_ref, y_ref = scratch
    i = pl.program_id(0)
    x = h_ref[...]
    xm = _rms(x, g_ref[0:1, :])
    g1, g2, g3 = POOL_GD, 2 * POOL_GD, 3 * POOL_GD

    @pl.when(i == 0)
    def _():
        for ref in (ext_ref, sa_ref, sb_ref):
            ref[0:_D_ZERO, :] = jnp.zeros((_D_ZERO, D), _F32)

    @pl.when(i < N_PROMPT_TILES)
    def _():
        @pl.when(i % TILES_PER_SEQ == 0)
        def _():
            ext_ref[_D_ZERO:_D_HIST, :] = jnp.zeros((_D_HIST - _D_ZERO, D), _F32)

        ext_ref[_D_HIST:_D_ROWS, :] = xm
        lo, hi = _D_ZERO, _D_ROWS
        sa_ref[lo:hi, :] = ext_ref[lo:hi, :] + ext_ref[lo - 1:hi - 1, :]
        sb_ref[lo:hi, g1:] = sa_ref[lo:hi, g1:] + sa_ref[lo - 2:hi - 2, g1:]
        sa_ref[lo:hi, g2:] = sb_ref[lo:hi, g2:] + sb_ref[lo - 4:hi - 4, g2:]
        tots = (sa_ref[_D_HIST:hi, 0:g1], sb_ref[_D_HIST:hi, g1:g2], sa_ref[_D_HIST:hi, g2:g3],
                sa_ref[_D_HIST:hi, g3:] + sa_ref[_D_HIST - 8:hi - 8, g3:])
        pos = (i % TILES_PER_SEQ) * TM + lax.broadcasted_iota(jnp.int32, (TM, 1), 0)
        for gi, w in enumerate(POOL_WINDOWS):
            lanes = slice(gi * POOL_GD, (gi + 1) * POOL_GD)
            cnt = jnp.minimum(w, pos + 1).astype(_F32)
            diff = tots[gi] / cnt - xm[:, lanes]
            y_ref[:, lanes] = _dot(diff.astype(_BF), wg_ref[gi * POOL_GD:(gi + 1) * POOL_GD, :])
        spp_ref[0] = ext_ref[_D_ROWS - POOL_PAD:_D_ROWS, :]
        ext_ref[_D_ZERO:_D_HIST, :] = ext_ref[TM + _D_ZERO:_D_ROWS, :]

    @pl.when(i == N_PROMPT_TILES)
    def _():
        xm_s_ref[...] = xm
        for gi, w in enumerate(POOL_WINDOWS):
            lanes = slice(gi * POOL_GD, (gi + 1) * POOL_GD)
            for l in range(DEC_SEQ):
                rows = slice(l * DEC_BATCH, (l + 1) * DEC_BATCH)
                cur = xm[rows, lanes]
                tot = cur + hp_ref[rows, lanes]
                for j in range(1, min(w - 1, l) + 1):
                    tot = tot + xm[(l - j) * DEC_BATCH:(l - j + 1) * DEC_BATCH, lanes]
                diff = tot / float(w) - cur
                y_ref[rows, lanes] = _dot(diff.astype(_BF), wg_ref[gi * POOL_GD:(gi + 1) * POOL_GD, :])

    m = y_ref[...] * scale_ref[...]
    o_ref[...] = x + _rms(m, g_ref[1:2, :])
    _run_casts(cast_in, cast_out)


def _mix_d(h, g2, wg, scale, hp, casts):
    return _call(
        _mix_d_body,
        grid=N_TILES,
        in_specs=[_tile_spec(), _const_spec((2, D)),
                  _const_spec((len(POOL_WINDOWS) * POOL_GD, POOL_GD)), _const_spec((1, D)),
                  _const_spec((TM, D))],
        args=(h, g2, wg, scale, hp),
        out_specs=[_tile_spec(),
                   pl.BlockSpec((1, POOL_PAD, D),
                                lambda i: (_prompt_tile_index(i) // TILES_PER_SEQ, 0, 0)),
                   pl.BlockSpec((TM, D), lambda i: (0, 0))],
        out_shapes=[jax.ShapeDtypeStruct((ROWS, D), _F32),
                    jax.ShapeDtypeStruct((BATCH, POOL_PAD, D), _F32),
                    jax.ShapeDtypeStruct((TM, D), _F32)],
        scratch_shapes=[pltpu.VMEM((_D_ROWS, D), _F32), pltpu.VMEM((_D_ROWS, D), _F32),
                        pltpu.VMEM((_D_ROWS, D), _F32), pltpu.VMEM((TM, D), _F32)],
        casts=casts, name="mix_d")


def _to_pos_major(a):
    return jnp.transpose(a, (1, 0, 2))


def _from_pos_major(a, length):
    return jnp.transpose(a.reshape(length, DEC_BATCH, D), (1, 0, 2))


def kernel(x_prompt, x_sample, state_conv_a, state_conv_b, state_pool, norm_g, ffn_w_in, ffn_w_down,
           a_w_in, a_conv_w, a_conv_b, a_ln_g, a_ln_b, a_w_out, b_w_in, b_conv_w, b_w_out,
           c_w_in, c_ln_g, c_ln_b, c_ws, c_bs, c_w_out, d_w_group, d_scale):
    row = lambda a: a.reshape(1, D)
    ffn_casts = lambda layer, half: _ffn_weight_casts(ffn_w_in, ffn_w_down, layer, half)
    mats = lambda *ws: [_matrix_cast(w) for w in ws]

    xp = x_prompt.reshape(BATCH * SEQ, D)
    xs = _to_pos_major(x_sample).reshape(TM, D)
    ha, hp = _hist(_to_pos_major(state_conv_a), a_conv_w, _to_pos_major(state_pool))

    (h,), (w_in, w_down, a_in, a_out) = _ffn(
        (xp, xs), norm_g[0, 0:2], ffn_w_in[0, 0].astype(_BF), ffn_w_down[0, 0].astype(_BF),
        ffn_casts(0, 1) + mats(a_w_in, a_w_out), split_in=True, name="ffn_0_0")
    (h, sa_p, glu_s), _ = _mix_a(
        h, norm_g[0, 2:4], a_in, a_conv_w, row(a_conv_b), row(a_ln_g), row(a_ln_b), a_out, ha, [])
    (h,), (w_in, w_down) = _ffn((h,), norm_g[0, 4:6], w_in, w_down, ffn_casts(1, 0), name="ffn_0_1")

    (h,), (w_in, w_down, b_in, b_out) = _ffn(
        (h,), norm_g[1, 0:2], w_in, w_down, ffn_casts(1, 1) + mats(b_w_in, b_w_out), name="ffn_1_0")
    (h, sb_p, z_s), _ = _mix_b(h, norm_g[1, 2:4], b_in, b_conv_w, b_out, _to_pos_major(state_conv_b), [])
    (h,), (w_in, w_down) = _ffn((h,), norm_g[1, 4:6], w_in, w_down, ffn_casts(2, 0), name="ffn_1_1")

    (h,), (w_in, w_down, c_in, c_out) = _ffn(
        (h,), norm_g[2, 0:2], w_in, w_down, ffn_casts(2, 1) + mats(c_w_in, c_w_out), name="ffn_2_0")
    wsv = jnp.repeat(jnp.transpose(c_ws[:, :DEC_SEQ, :DEC_SEQ], (1, 2, 0)), _HD, axis=-1)
    bsv = jnp.repeat(jnp.transpose(c_bs[:, :DEC_SEQ]), _HD, axis=-1)
    (h, v_s), _ = _mix_c(h, norm_g[2, 2:4], c_in, row(c_ln_g), row(c_ln_b), c_ws, jnp.transpose(c_bs),
                         wsv, bsv, c_out, [])
    (h,), (w_in, w_down) = _ffn((h,), norm_g[2, 4:6], w_in, w_down, ffn_casts(3, 0), name="ffn_2_1")

    (h,), (w_in, w_down, d_wg) = _ffn(
        (h,), norm_g[3, 0:2], w_in, w_down,
        ffn_casts(3, 1) + mats(d_w_group.reshape(len(POOL_WINDOWS) * POOL_GD, POOL_GD)), name="ffn_3_0")
    (h, sp_p, xm_s), _ = _mix_d(h, norm_g[3, 2:4], d_wg, row(d_scale), hp, [])
    (yp, ys), _ = _ffn((h,), norm_g[3, 4:6], w_in, w_down, split_out=True, name="ffn_3_1")

    y_prompt = yp.reshape(BATCH, SEQ, D)
    y_sample = _from_pos_major(ys, DEC_SEQ)
    sa_s = jnp.concatenate([state_conv_a[:, DEC_SEQ:], _from_pos_major(glu_s, DEC_SEQ)], axis=1)
    sb_s = _from_pos_major(z_s, CONV_B - 1)
    sc_s = _from_pos_major(v_s, DEC_SEQ)
    sp_s = jnp.concatenate([state_pool[:, DEC_SEQ:], _from_pos_major(xm_s, DEC_SEQ)], axis=1)
    return (y_prompt, y_sample, sa_p, sa_s, sb_p, sb_s, sc_s, sp_p, sp_s)
```

```python
import functools
from typing import Any, NamedTuple

import jax
import jax.numpy as jnp
from jax import lax
from jax.experimental import pallas as pl
from jax.experimental.pallas import tpu as pltpu

D = 1024
D_FF = 2816
DEPTH = 4
EPS = 1e-6
FFN_HALF = 0.5
CONV_A = 31
CONV_B = 3
CHUNK = 128
HEADS = 8
POOL_WINDOWS = (2, 4, 8, 16)
POOL_GD = D // len(POOL_WINDOWS)
POOL_PAD = 15

BATCH = 8
SEQ = 2048
DEC_BATCH = 128
DEC_SEQ = 4

TM = 512
TILES_PER_SEQ = SEQ // TM
N_PROMPT_TILES = BATCH * SEQ // TM
N_TILES = N_PROMPT_TILES + 1
ROWS = N_TILES * TM
FF_CHUNK = 256
SUBLANES = 8
LANES = 128

FFN_CAST_STEPS = 22
MIX_CAST_STEPS = 8

_BF = jnp.bfloat16
_F32 = jnp.float32
_VMEM_LIMIT = 60 * 1024 * 1024


def _dot(a, b):
    return jnp.dot(a, b, preferred_element_type=_F32)


def _rms(x, g):
    return x * lax.rsqrt(jnp.mean(x * x, axis=-1, keepdims=True) + EPS) * g


def _ln(x, g, b):
    xc = x - jnp.mean(x, axis=-1, keepdims=True)
    return xc * lax.rsqrt(jnp.mean(xc * xc, axis=-1, keepdims=True) + EPS) * g + b


def _gelu(x):
    return 0.5 * x * (1.0 + lax.erf(x * (2.0 ** -0.5)))


def _const_spec(shape):
    zeros = (0,) * len(shape)
    return pl.BlockSpec(shape, lambda i: zeros, pipeline_mode=pl.Buffered(1))


def _tile_spec():
    return pl.BlockSpec((TM, D), lambda i: (i, 0))


def _prompt_tile_index(i):
    return jnp.minimum(i, N_PROMPT_TILES - 1)


def _params():
    return pltpu.CompilerParams(dimension_semantics=("arbitrary",),
                                vmem_limit_bytes=_VMEM_LIMIT)


class _Cast(NamedTuple):
    src: Any
    in_spec: Any
    out_spec: Any
    out_shape: Any


def _ffn_weight_casts(ffn_w_in, ffn_w_down, layer, half):
    n = FFN_CAST_STEPS
    cb = 2 * D_FF // n
    rb = D_FF // n
    step = lambda i: jnp.minimum(i, n - 1)
    return [
        _Cast(ffn_w_in,
              pl.BlockSpec((None, None, D, cb), lambda i: (layer, half, 0, step(i))),
              pl.BlockSpec((D, cb), lambda i: (0, step(i))),
              jax.ShapeDtypeStruct((D, 2 * D_FF), _BF)),
        _Cast(ffn_w_down,
              pl.BlockSpec((None, None, rb, D), lambda i: (layer, half, step(i), 0)),
              pl.BlockSpec((rb, D), lambda i: (step(i), 0)),
              jax.ShapeDtypeStruct((D_FF, D), _BF)),
    ]


def _matrix_cast(w):
    n = MIX_CAST_STEPS
    rows, cols = w.shape
    rb = rows // n
    step = lambda i: jnp.minimum(i, n - 1)
    return _Cast(w,
                 pl.BlockSpec((rb, cols), lambda i: (step(i), 0)),
                 pl.BlockSpec((rb, cols), lambda i: (step(i), 0)),
                 jax.ShapeDtypeStruct((rows, cols), _BF))


def _run_casts(in_refs, out_refs):
    for src_ref, dst_ref in zip(in_refs, out_refs):
        dst_ref[...] = src_ref[...].astype(_BF)


def _unpack(refs, n_in, n_out, n_cast):
    refs = list(refs)
    o0 = n_in + n_cast
    s0 = o0 + n_out + n_cast
    return refs[:n_in], refs[n_in:o0], refs[o0:o0 + n_out], refs[o0 + n_out:s0], refs[s0:]


def _call(body, *, grid, in_specs, args, out_specs, out_shapes, casts, scratch_shapes=(), name):
    outs = pl.pallas_call(
        functools.partial(body, n_cast=len(casts)),
        grid=(grid,),
        in_specs=list(in_specs) + [c.in_spec for c in casts],
        out_specs=list(out_specs) + [c.out_spec for c in casts],
        out_shape=list(out_shapes) + [c.out_shape for c in casts],
        scratch_shapes=list(scratch_shapes),
        compiler_params=_params(),
        name=name,
    )(*args, *[c.src for c in casts])
    n_out = len(out_specs)
    return outs[:n_out], outs[n_out:]


def _ffn_math(x, g_ref, w_in_ref, w_down_ref):
    xn = _rms(x, g_ref[0:1, :]).astype(_BF)
    acc = None
    for c in range(D_FF // FF_CHUNK):
        c0 = c * FF_CHUNK
        gate = _dot(xn, w_in_ref[:, c0:c0 + FF_CHUNK])
        up = _dot(xn, w_in_ref[:, D_FF + c0:D_FF + c0 + FF_CHUNK])
        act = (gate * jax.nn.sigmoid(gate) * up).astype(_BF)
        d = _dot(act, w_down_ref[c0:c0 + FF_CHUNK, :])
        acc = d if acc is None else acc + d
    return x + FFN_HALF * _rms(acc, g_ref[1:2, :])


def _ffn_body(*refs, split_in, split_out, n_cast):
    n_x = 2 if split_in else 1
    ins, cast_in, outs, cast_out, _ = _unpack(refs, n_x + 3, 2 if split_out else 1, n_cast)
    x_refs = ins[:n_x]
    g_ref, w_in_ref, w_down_ref = ins[n_x:]
    i = pl.program_id(0)
    if split_in:
        x = jnp.where(i < N_PROMPT_TILES, x_refs[0][...], x_refs[1][...])
    else:
        x = x_refs[0][...]
    y = _ffn_math(x, g_ref, w_in_ref, w_down_ref)
    if split_out:
        @pl.when(i < N_PROMPT_TILES)
        def _():
            outs[0][...] = y

        @pl.when(i == N_PROMPT_TILES)
        def _():
            outs[1][...] = y
    else:
        outs[0][...] = y
    _run_casts(cast_in, cast_out)


def _ffn(xs, g2, w_in, w_down, casts=(), *, split_in=False, split_out=False, name):
    split_specs = [pl.BlockSpec((TM, D), lambda i: (_prompt_tile_index(i), 0)),
                   pl.BlockSpec((TM, D), lambda i: (0, 0))]
    split_shapes = [jax.ShapeDtypeStruct((N_PROMPT_TILES * TM, D), _F32),
                    jax.ShapeDtypeStruct((TM, D), _F32)]
    return _call(
        functools.partial(_ffn_body, split_in=split_in, split_out=split_out),
        grid=N_TILES,
        in_specs=(split_specs if split_in else [_tile_spec()])
        + [_const_spec((2, D)), _const_spec((D, 2 * D_FF)), _const_spec((D_FF, D))],
        args=(*xs, g2, w_in, w_down),
        out_specs=split_specs if split_out else [_tile_spec()],
        out_shapes=split_shapes if split_out else [jax.ShapeDtypeStruct((ROWS, D), _F32)],
        casts=list(casts), name=name)


_HIST_BB = 32


def _hist_body(sta_ref, cw_ref, stp_ref, ha_ref, hp_ref):
    for l in range(DEC_SEQ):
        acc = jnp.zeros((_HIST_BB, D), _F32)
        for j in range(l, CONV_A - 1):
            acc = acc + sta_ref[j] * cw_ref[j - l:j - l + 1, :]
        ha_ref[l] = acc
        for gi, w in enumerate(POOL_WINDOWS):
            lanes = slice(gi * POOL_GD, (gi + 1) * POOL_GD)
            tot = jnp.zeros((_HIST_BB, POOL_GD), _F32)
            for j in range(l + 1, w):
                tot = tot + stp_ref[POOL_PAD + l - j, :, lanes]
            hp_ref[l, :, lanes] = tot


def _hist(st_a_t, cw, st_p_t):
    out = jax.ShapeDtypeStruct((DEC_SEQ, DEC_BATCH, D), _F32)
    ha, hp = pl.pallas_call(
        _hist_body,
        grid=(DEC_BATCH // _HIST_BB,),
        in_specs=[pl.BlockSpec((CONV_A - 1, _HIST_BB, D), lambda i: (0, i, 0)),
                  pl.BlockSpec((CONV_A, D), lambda i: (0, 0)),
                  pl.BlockSpec((POOL_PAD, _HIST_BB, D), lambda i: (0, i, 0))],
        out_specs=[pl.BlockSpec((DEC_SEQ, _HIST_BB, D), lambda i: (0, i, 0)),
                   pl.BlockSpec((DEC_SEQ, _HIST_BB, D), lambda i: (0, i, 0))],
        out_shape=[out, out],
        compiler_params=_params(),
        name="sample_hist",
    )(st_a_t, cw, st_p_t)
    return ha.reshape(TM, D), hp.reshape(TM, D)


_A_HIST = 32
_A_FIRST = _A_HIST - (CONV_A - 1)
_A_LB = 256
_A_RB = 64
_A_SHIFT_ROWS = 536


def _mix_a_body(*refs, n_cast):
    ins, cast_in, outs, cast_out, scratch = _unpack(refs, 9, 3, n_cast)
    h_ref, g_ref, w_in_ref, cw_ref, cb_ref, lng_ref, lnb_ref, w_out_ref, ha_ref = ins
    o_ref, sap_ref, glu_s_ref = outs
    ext_ref, cwb_ref, c_ref, sh_ref = scratch
    i = pl.program_id(0)

    @pl.when(i == 0)
    def _():
        for k in range(CONV_A):
            cwb_ref[k] = jnp.broadcast_to(cw_ref[k:k + 1, :], (SUBLANES, D))

    x = h_ref[...]
    xm = _rms(x, g_ref[0:1, :]).astype(_BF)
    glu = _dot(xm, w_in_ref[:, :D]) * jax.nn.sigmoid(_dot(xm, w_in_ref[:, D:]))

    def blocks(v, rows, lanes):
        return v.reshape(rows // SUBLANES, SUBLANES, lanes)

    ext_ref[_A_HIST:_A_HIST + TM, :] = glu

    @pl.when(i < N_PROMPT_TILES)
    def _():
        @pl.when(i % TILES_PER_SEQ == 0)
        def _():
            ext_ref[0:_A_HIST, :] = jnp.zeros((_A_HIST, D), _F32)

        for lb in range(D // _A_LB):
            lanes = slice(lb * _A_LB, (lb + 1) * _A_LB)
            for r in range(1, SUBLANES):
                sh_ref[r - 1] = ext_ref[r:r + _A_SHIFT_ROWS, lanes]
            bias = jnp.broadcast_to(cb_ref[:, lanes], (SUBLANES, _A_LB))[None]
            for rb in range(TM // _A_RB):
                r0 = rb * _A_RB
                acc = jnp.broadcast_to(bias, (_A_RB // SUBLANES, SUBLANES, _A_LB))
                for k in range(CONV_A):
                    r = (_A_FIRST + k) % SUBLANES
                    s = r0 + _A_FIRST + k - r
                    if r == 0:
                        rows = ext_ref[s:s + _A_RB, lanes]
                    else:
                        rows = sh_ref[r - 1, s:s + _A_RB, :]
                    acc = acc + blocks(rows, _A_RB, _A_LB) * cwb_ref[k][:, lanes][None]
                c_ref[r0:r0 + _A_RB, lanes] = acc.reshape(_A_RB, _A_LB)
        sap_ref[0] = ext_ref[TM + _A_FIRST:TM + _A_HIST, :]
        ext_ref[0:_A_HIST, :] = ext_ref[TM:TM + _A_HIST, :]

    @pl.when(i == N_PROMPT_TILES)
    def _():
        glu_s_ref[...] = glu
        ext_ref[0:TM, :] = glu
        bias = jnp.broadcast_to(cb_ref[...], (SUBLANES, D))[None]
        for l in range(DEC_SEQ):
            for rb in range(DEC_BATCH // _A_RB):
                r0 = l * DEC_BATCH + rb * _A_RB
                acc = blocks(ha_ref[r0:r0 + _A_RB, :], _A_RB, D) + bias
                for lp in range(l + 1):
                    s = lp * DEC_BATCH + rb * _A_RB
                    acc = acc + blocks(ext_ref[s:s + _A_RB, :], _A_RB, D) * cwb_ref[CONV_A - 1 + lp - l][None]
                c_ref[r0:r0 + _A_RB, :] = acc.reshape(_A_RB, D)

    c = _ln(c_ref[...], lng_ref[...], lnb_ref[...])
    c = c * jax.nn.sigmoid(c)
    m = _dot(c.astype(_BF), w_out_ref[...])
    o_ref[...] = x + _rms(m, g_ref[1:2, :])
    _run_casts(cast_in, cast_out)


def _mix_a(h, g2, w_in, cw, cb, lng, lnb, w_out, ha, casts):
    return _call(
        _mix_a_body,
        grid=N_TILES,
        in_specs=[_tile_spec(), _const_spec((2, D)), _const_spec((D, 2 * D)),
                  _const_spec((CONV_A, D)), _const_spec((1, D)), _const_spec((1, D)),
                  _const_spec((1, D)), _const_spec((D, D)), _const_spec((TM, D))],
        args=(h, g2, w_in, cw, cb, lng, lnb, w_out, ha),
        out_specs=[_tile_spec(),
                   pl.BlockSpec((1, CONV_A - 1, D),
                                lambda i: (_prompt_tile_index(i) // TILES_PER_SEQ, 0, 0)),
                   pl.BlockSpec((TM, D), lambda i: (0, 0))],
        out_shapes=[jax.ShapeDtypeStruct((ROWS, D), _F32),
                    jax.ShapeDtypeStruct((BATCH, CONV_A - 1, D), _F32),
                    jax.ShapeDtypeStruct((TM, D), _F32)],
        scratch_shapes=[pltpu.VMEM((_A_HIST + TM, D), _F32),
                        pltpu.VMEM((CONV_A, SUBLANES, D), _F32),
                        pltpu.VMEM((TM, D), _F32),
                        pltpu.VMEM((SUBLANES - 1, _A_SHIFT_ROWS, _A_LB), _F32)],
        casts=casts, name="mix_a")


_B_HIST = 8


def _mix_b_body(*refs, n_cast):
    ins, cast_in, outs, cast_out, scratch = _unpack(refs, 6, 3, n_cast)
    h_ref, g_ref, w_in_ref, cw_ref, w_out_ref, st_ref = ins
    o_ref, sbp_ref, z_s_ref = outs
    ext_ref, c_ref = scratch
    i = pl.program_id(0)
    x = h_ref[...]
    xm = _rms(x, g_ref[0:1, :]).astype(_BF)
    z = _dot(xm, w_in_ref[:, D:2 * D]) * _dot(xm, w_in_ref[:, 2 * D:])
    b_gate = _dot(xm, w_in_ref[:, :D])
    w0 = cw_ref[0:1, :]
    w1 = cw_ref[1:2, :]
    w2 = cw_ref[2:3, :]

    ext_ref[_B_HIST:_B_HIST + TM, :] = z

    @pl.when(i < N_PROMPT_TILES)
    def _():
        @pl.when(i % TILES_PER_SEQ == 0)
        def _():
            ext_ref[0:_B_HIST, :] = jnp.zeros((_B_HIST, D), _F32)

        c_ref[...] = (ext_ref[_B_HIST - 2:_B_HIST - 2 + TM, :] * w0
                      + ext_ref[_B_HIST - 1:_B_HIST - 1 + TM, :] * w1
                      + z * w2)
        sbp_ref[0] = z[TM - (CONV_B - 1):, :]
        ext_ref[0:_B_HIST, :] = ext_ref[TM:TM + _B_HIST, :]

    @pl.when(i == N_PROMPT_TILES)
    def _():
        zs = [st_ref[0], st_ref[1]] + [z[l * DEC_BATCH:(l + 1) * DEC_BATCH, :] for l in range(DEC_SEQ)]
        for l in range(DEC_SEQ):
            c_ref[l * DEC_BATCH:(l + 1) * DEC_BATCH, :] = zs[l] * w0 + zs[l + 1] * w1 + zs[l + 2] * w2
        z_s_ref[...] = z[(DEC_SEQ - (CONV_B - 1)) * DEC_BATCH:, :]

    m = _dot((b_gate * c_ref[...]).astype(_BF), w_out_ref[...])
    o_ref[...] = x + _rms(m, g_ref[1:2, :])
    _run_casts(cast_in, cast_out)


def _mix_b(h, g2, w_in, cw, w_out, st_t, casts):
    n_keep = CONV_B - 1
    return _call(
        _mix_b_body,
        grid=N_TILES,
        in_specs=[_tile_spec(), _const_spec((2, D)), _const_spec((D, 3 * D)),
                  _const_spec((CONV_B, D)), _const_spec((D, D)),
                  _const_spec((n_keep, DEC_BATCH, D))],
        args=(h, g2, w_in, cw, w_out, st_t),
        out_specs=[_tile_spec(),
                   pl.BlockSpec((1, n_keep, D),
                                lambda i: (_prompt_tile_index(i) // TILES_PER_SEQ, 0, 0)),
                   pl.BlockSpec((n_keep * DEC_BATCH, D), lambda i: (0, 0))],
        out_shapes=[jax.ShapeDtypeStruct((ROWS, D), _F32),
                    jax.ShapeDtypeStruct((BATCH, n_keep, D), _F32),
                    jax.ShapeDtypeStruct((n_keep * DEC_BATCH, D), _F32)],
        scratch_shapes=[pltpu.VMEM((_B_HIST + TM, D), _F32),
                        pltpu.VMEM((TM, D), _F32)],
        casts=casts, name="mix_b")


_HD = D // HEADS


def _mix_c_body(*refs, n_cast):
    ins, cast_in, outs, cast_out, scratch = _unpack(refs, 10, 2, n_cast)
    h_ref, g_ref, w_in_ref, lng_ref, lnb_ref, ws_ref, bst_ref, wsv_ref, bsv_ref, w_out_ref = ins
    o_ref, v_s_ref = outs
    (s_ref,) = scratch
    i = pl.program_id(0)
    x = h_ref[...]
    xm = _rms(x, g_ref[0:1, :]).astype(_BF)
    v = _ln(_gelu(_dot(xm, w_in_ref[:, D:])), lng_ref[...], lnb_ref[...])
    u = _gelu(_dot(xm, w_in_ref[:, :D]))

    @pl.when(i < N_PROMPT_TILES)
    def _():
        vb = v.astype(_BF)
        n_chunks = TM // CHUNK
        row = lax.broadcasted_iota(jnp.int32, (CHUNK, CHUNK), 0)
        col = lax.broadcasted_iota(jnp.int32, (CHUNK, CHUNK), 1)
        causal = col <= row
        for hd in range(HEADS):
            lanes = slice(hd * _HD, (hd + 1) * _HD)
            wm = jnp.where(causal, ws_ref[hd], 0.0).astype(_BF)
            rhs = jnp.concatenate([vb[n * CHUNK:(n + 1) * CHUNK, lanes] for n in range(n_chunks)], axis=1)
            out = _dot(wm, rhs) + bst_ref[:, hd:hd + 1]
            for n in range(n_chunks):
                s_ref[n * CHUNK:(n + 1) * CHUNK, lanes] = out[:, n * CHUNK:(n + 1) * CHUNK]

    @pl.when(i == N_PROMPT_TILES)
    def _():
        v_s_ref[...] = v
        for l in range(DEC_SEQ):
            acc = jnp.broadcast_to(bsv_ref[l:l + 1, :], (DEC_BATCH, D))
            for lp in range(l + 1):
                acc = acc + v[lp * DEC_BATCH:(lp + 1) * DEC_BATCH, :] * wsv_ref[l, lp:lp + 1, :]
            s_ref[l * DEC_BATCH:(l + 1) * DEC_BATCH, :] = acc

    m = _dot((u * s_ref[...]).astype(_BF), w_out_ref[...])
    o_ref[...] = x + _rms(m, g_ref[1:2, :])
    _run_casts(cast_in, cast_out)


def _mix_c(h, g2, w_in, lng, lnb, ws, bst, wsv, bsv, w_out, casts):
    return _call(
        _mix_c_body,
        grid=N_TILES,
        in_specs=[_tile_spec(), _const_spec((2, D)), _const_spec((D, 2 * D)),
                  _const_spec((1, D)), _const_spec((1, D)),
                  _const_spec((HEADS, CHUNK, CHUNK)), _const_spec((CHUNK, HEADS)),
                  _const_spec((DEC_SEQ, DEC_SEQ, D)), _const_spec((DEC_SEQ, D)),
                  _const_spec((D, D))],
        args=(h, g2, w_in, lng, lnb, ws, bst, wsv, bsv, w_out),
        out_specs=[_tile_spec(), pl.BlockSpec((TM, D), lambda i: (0, 0))],
        out_shapes=[jax.ShapeDtypeStruct((ROWS, D), _F32),
                    jax.ShapeDtypeStruct((TM, D), _F32)],
        scratch_shapes=[pltpu.VMEM((TM, D), _F32)],
        casts=casts, name="mix_c")


_D_ZERO = 8
_D_HIST = 24
_D_ROWS = _D_HIST + TM


def _mix_d_body(*refs, n_cast):
    ins, cast_in, outs, cast_out, scratch = _unpack(refs, 5, 3, n_cast)
    h_ref, g_ref, wg_ref, scale_ref, hp_ref = ins
    o_ref, spp_ref, xm_s_ref = outs
    ext_ref, sa_ref, sb_ref, y_ref = scratch
    i = pl.program_id(0)
    x = h_ref[...]
    xm = _rms(x, g_ref[0:1, :])
    g1, g2, g3 = POOL_GD, 2 * POOL_GD, 3 * POOL_GD

    @pl.when(i == 0)
    def _():
        for ref in (ext_ref, sa_ref, sb_ref):
            ref[0:_D_ZERO, :] = jnp.zeros((_D_ZERO, D), _F32)

    @pl.when(i < N_PROMPT_TILES)
    def _():
        @pl.when(i % TILES_PER_SEQ == 0)
        def _():
            ext_ref[_D_ZERO:_D_HIST, :] = jnp.zeros((_D_HIST - _D_ZERO, D), _F32)

        ext_ref[_D_HIST:_D_ROWS, :] = xm
        lo, hi = _D_ZERO, _D_ROWS
        sa_ref[lo:hi, :] = ext_ref[lo:hi, :] + ext_ref[lo - 1:hi - 1, :]
        sb_ref[lo:hi, g1:] = sa_ref[lo:hi, g1:] + sa_ref[lo - 2:hi - 2, g1:]
        sa_ref[lo:hi, g2:] = sb_ref[lo:hi, g2:] + sb_ref[lo - 4:hi - 4, g2:]
        tots = (sa_ref[_D_HIST:hi, 0:g1], sb_ref[_D_HIST:hi, g1:g2], sa_ref[_D_HIST:hi, g2:g3],
                sa_ref[_D_HIST:hi, g3:] + sa_ref[_D_HIST - 8:hi - 8, g3:])
        pos = (i % TILES_PER_SEQ) * TM + lax.broadcasted_iota(jnp.int32, (TM, 1), 0)
        for gi, w in enumerate(POOL_WINDOWS):
            lanes = slice(gi * POOL_GD, (gi + 1) * POOL_GD)
            cnt = jnp.minimum(w, pos + 1).astype(_F32)
            diff = tots[gi] / cnt - xm[:, lanes]
            y_ref[:, lanes] = _dot(diff.astype(_BF), wg_ref[gi * POOL_GD:(gi + 1) * POOL_GD, :])
        spp_ref[0] = ext_ref[_D_ROWS - POOL_PAD:_D_ROWS, :]
        ext_ref[_D_ZERO:_D_HIST, :] = ext_ref[TM + _D_ZERO:_D_ROWS, :]

    @pl.when(i == N_PROMPT_TILES)
    def _():
        xm_s_ref[...] = xm
        for gi, w in enumerate(POOL_WINDOWS):
            lanes = slice(gi * POOL_GD, (gi + 1) * POOL_GD)
            for l in range(DEC_SEQ):
                rows = slice(l * DEC_BATCH, (l + 1) * DEC_BATCH)
                cur = xm[rows, lanes]
                tot = cur + hp_ref[rows, lanes]
                for j in range(1, min(w - 1, l) + 1):
                    tot = tot + xm[(l - j) * DEC_BATCH:(l - j + 1) * DEC_BATCH, lanes]
                diff = tot / float(w) - cur
                y_ref[rows, lanes] = _dot(diff.astype(_BF), wg_ref[gi * POOL_GD:(gi + 1) * POOL_GD, :])

    m = y_ref[...] * scale_ref[...]
    o_ref[...] = x + _rms(m, g_ref[1:2, :])
    _run_casts(cast_in, cast_out)


def _mix_d(h, g2, wg, scale, hp, casts):
    return _call(
        _mix_d_body,
        grid=N_TILES,
        in_specs=[_tile_spec(), _const_spec((2, D)),
                  _const_spec((len(POOL_WINDOWS) * POOL_GD, POOL_GD)), _const_spec((1, D)),
                  _const_spec((TM, D))],
        args=(h, g2, wg, scale, hp),
        out_specs=[_tile_spec(),
                   pl.BlockSpec((1, POOL_PAD, D),
                                lambda i: (_prompt_tile_index(i) // TILES_PER_SEQ, 0, 0)),
                   pl.BlockSpec((TM, D), lambda i: (0, 0))],
        out_shapes=[jax.ShapeDtypeStruct((ROWS, D), _F32),
                    jax.ShapeDtypeStruct((BATCH, POOL_PAD, D), _F32),
                    jax.ShapeDtypeStruct((TM, D), _F32)],
        scratch_shapes=[pltpu.VMEM((_D_ROWS, D), _F32), pltpu.VMEM((_D_ROWS, D), _F32),
                        pltpu.VMEM((_D_ROWS, D), _F32), pltpu.VMEM((TM, D), _F32)],
        casts=casts, name="mix_d")


def _to_pos_major(a):
    return jnp.transpose(a, (1, 0, 2))


def _from_pos_major(a, length):
    return jnp.transpose(a.reshape(length, DEC_BATCH, D), (1, 0, 2))


def kernel(x_prompt, x_sample, state_conv_a, state_conv_b, state_pool, norm_g, ffn_w_in, ffn_w_down,
           a_w_in, a_conv_w, a_conv_b, a_ln_g, a_ln_b, a_w_out, b_w_in, b_conv_w, b_w_out,
           c_w_in, c_ln_g, c_ln_b, c_ws, c_bs, c_w_out, d_w_group, d_scale):
    row = lambda a: a.reshape(1, D)
    ffn_casts = lambda layer, half: _ffn_weight_casts(ffn_w_in, ffn_w_down, layer, half)
    mats = lambda *ws: [_matrix_cast(w) for w in ws]

    xp = x_prompt.reshape(BATCH * SEQ, D)
    xs = _to_pos_major(x_sample).reshape(TM, D)
    ha, hp = _hist(_to_pos_major(state_conv_a), a_conv_w, _to_pos_major(state_pool))

    (h,), (w_in, w_down, a_in, a_out) = _ffn(
        (xp, xs), norm_g[0, 0:2], ffn_w_in[0, 0].astype(_BF), ffn_w_down[0, 0].astype(_BF),
        ffn_casts(0, 1) + mats(a_w_in, a_w_out), split_in=True, name="ffn_0_0")
    (h, sa_p, glu_s), _ = _mix_a(
        h, norm_g[0, 2:4], a_in, a_conv_w, row(a_conv_b), row(a_ln_g), row(a_ln_b), a_out, ha, [])
    (h,), (w_in, w_down) = _ffn((h,), norm_g[0, 4:6], w_in, w_down, ffn_casts(1, 0), name="ffn_0_1")

    (h,), (w_in, w_down, b_in, b_out) = _ffn(
        (h,), norm_g[1, 0:2], w_in, w_down, ffn_casts(1, 1) + mats(b_w_in, b_w_out), name="ffn_1_0")
    (h, sb_p, z_s), _ = _mix_b(h, norm_g[1, 2:4], b_in, b_conv_w, b_out, _to_pos_major(state_conv_b), [])
    (h,), (w_in, w_down) = _ffn((h,), norm_g[1, 4:6], w_in, w_down, ffn_casts(2, 0), name="ffn_1_1")

    (h,), (w_in, w_down, c_in, c_out) = _ffn(
        (h,), norm_g[2, 0:2], w_in, w_down, ffn_casts(2, 1) + mats(c_w_in, c_w_out), name="ffn_2_0")
    wsv = jnp.repeat(jnp.transpose(c_ws[:, :DEC_SEQ, :DEC_SEQ], (1, 2, 0)), _HD, axis=-1)
    bsv = jnp.repeat(jnp.transpose(c_bs[:, :DEC_SEQ]), _HD, axis=-1)
    (h, v_s), _ = _mix_c(h, norm_g[2, 2:4], c_in, row(c_ln_g), row(c_ln_b), c_ws, jnp.transpose(c_bs),
                         wsv, bsv, c_out, [])
    (h,), (w_in, w_down) = _ffn((h,), norm_g[2, 4:6], w_in, w_down, ffn_casts(3, 0), name="ffn_2_1")

    (h,), (w_in, w_down, d_wg) = _ffn(
        (h,), norm_g[3, 0:2], w_in, w_down,
        ffn_casts(3, 1) + mats(d_w_group.reshape(len(POOL_WINDOWS) * POOL_GD, POOL_GD)), name="ffn_3_0")
    (h, sp_p, xm_s), _ = _mix_d(h, norm_g[3, 2:4], d_wg, row(d_scale), hp, [])
    (yp, ys), _ = _ffn((h,), norm_g[3, 4:6], w_in, w_down, split_out=True, name="ffn_3_1")

    y_prompt = yp.reshape(BATCH, SEQ, D)
    y_sample = _from_pos_major(ys, DEC_SEQ)
    sa_s = jnp.concatenate([state_conv_a[:, DEC_SEQ:], _from_pos_major(glu_s, DEC_SEQ)], axis=1)
    sb_s = _from_pos_major(z_s, CONV_B - 1)
    sc_s = _from_pos_major(v_s, DEC_SEQ)
    sp_s = jnp.concatenate([state_pool[:, DEC_SEQ:], _from_pos_major(xm_s, DEC_SEQ)], axis=1)
    return (y_prompt, y_sample, sa_p, sa_s, sb_p, sb_s, sc_s, sp_p, sp_s)
```
